```python
import math, functools
import jax
import jax.numpy as jnp
from jax import lax
import numpy as np

D_MODEL = 1024
BATCH = 4
SEQ = 8192
DEPTH = 2

GRID_W = 64
CTX_LEN = 256
N_MIXERS = 4
D_MIX = D_MODEL
D_GROUP = D_MIX // N_MIXERS
CHUNK = 64
SHORT_CONV = 3
N_DIR = 2
EPS = 1e-6
N_MOD = 6

GLA_HEADS = 4
GLA_DK = D_GROUP // (2 * GLA_HEADS)
GLA_DV = D_GROUP // GLA_HEADS
GLA_RANK = 16
GLA_TAU = 16.0

HY_ORDER = 2
HY_EMB = 33
HY_BANDS = (HY_EMB - 1) // 2
HY_FILTER_HIDDEN = 64
HY_DECAY_TARGET = 1e-2
HY_FAST_DECAY = 0.3
HY_SLOW_DECAY = 1.5

MB_HEADS = 4
MB_HEADDIM = D_GROUP // MB_HEADS
MB_STATE = 128
MB_GROUPS = 2
MB_CONV_CH = D_GROUP + 2 * MB_GROUPS * MB_STATE

DN_HEADS = 4
DN_DK = D_GROUP // DN_HEADS
DN_DV = D_GROUP // DN_HEADS

N_EXPERTS = 16
EC_CAPACITY = 2
D_EXPERT = 1024

GLA_COLS = 2 * GLA_HEADS * GLA_DK + 2 * D_GROUP + N_DIR * GLA_RANK
HY_COLS = (HY_ORDER + 1) * D_GROUP
MB_COLS = D_GROUP + MB_CONV_CH + N_DIR * MB_HEADS
DN_COLS = 4 * D_GROUP + 2 * N_DIR * DN_HEADS
D_IN_PROJ = GLA_COLS + HY_COLS + MB_COLS + DN_COLS

kernel_name = 'hybrid_parallel_heads_ec_moe_diffusion'

F32 = jnp.float32


def rmsnorm(x, w):
    xf = x.astype(F32)
    y = xf * lax.rsqrt(jnp.mean(xf * xf, axis=-1, keepdims=True) + EPS)
    return (y * w.astype(F32)).astype(x.dtype)


def head_rmsnorm(o, w, n_heads):
    b_, l_, hd = o.shape
    oh = o.reshape(b_, l_, n_heads, hd // n_heads)
    oh = oh * lax.rsqrt(jnp.mean(oh * oh, axis=-1, keepdims=True) + EPS)
    return oh.reshape(b_, l_, hd) * w.astype(F32)


def l2norm(t):
    return t * lax.rsqrt(jnp.sum(t * t, axis=-1, keepdims=True) + EPS)


def to_heads(t, n_heads):
    b_, l_, hd = t.shape
    return t.reshape(b_, l_, n_heads, hd // n_heads).transpose(0, 2, 1, 3).astype(F32)


def from_heads(t):
    b_, h_, l_, d_ = t.shape
    return t.transpose(0, 2, 1, 3).reshape(b_, l_, h_ * d_)


def to_col_major(u, rows):
    b_, l_, ch = u.shape
    return u.reshape(b_, rows, GRID_W, ch).transpose(0, 2, 1, 3).reshape(b_, l_, ch)


def to_row_major(u, rows):
    b_, l_, ch = u.shape
    return u.reshape(b_, GRID_W, rows, ch).transpose(0, 2, 1, 3).reshape(b_, l_, ch)


def short_conv(u, w, b=None):
    k_, ch = w.shape
    y = lax.conv_general_dilated(u, w[:, None, :].astype(u.dtype), window_strides=(1,),
                                 padding=[(k_ // 2, k_ // 2)], dimension_numbers=('NWC', 'WIO', 'NWC'),
                                 feature_group_count=ch)
    if b is not None:
        y = y + b.astype(u.dtype)
    return y


def modulation(cond, w_mod, b_mod):
    m = jax.nn.silu(cond.astype(F32)) @ w_mod.astype(F32) + b_mod.astype(F32)
    return jnp.split(m, N_MOD, axis=-1)


def modulate(x, g, shift, scale):
    return (rmsnorm(x, g).astype(F32) * (1.0 + scale) + shift).astype(x.dtype)


def bidirectional_scan(scan_f, scan_b, ctx_in, lat_in, s0, t_axis, need_ctx):
    rev = lambda ts: tuple(jnp.flip(t, t_axis) for t in ts)
    oc_f, s_f = scan_f(*ctx_in[0], s0)
    oc_b, s_b = scan_b(*rev(ctx_in[1]), s0)
    ol_f, _ = scan_f(*lat_in[0], s_f)
    ol_b, _ = scan_b(*rev(lat_in[1]), s_b)
    o_lat = ol_f + jnp.flip(ol_b, t_axis)
    o_ctx = (oc_f + jnp.flip(oc_b, t_axis)) if need_ctx else None
    return o_ctx, o_lat


def gla_scan(q, k, v, g, s0):
    b_, h_, l_, dk = q.shape
    dv = v.shape[-1]
    n = l_ // CHUNK
    q, k, v, g = [t.reshape(b_, h_, n, CHUNK, t.shape[-1]) for t in (q, k, v, g)]
    cum = jnp.cumsum(g, axis=-2)
    cum_last = cum[..., -1:, :]
    q_g = q * jnp.exp(cum)
    k_g = k * jnp.exp(-cum)
    k_end = k * jnp.exp(cum_last - cum)
    causal = jnp.tril(jnp.ones((CHUNK, CHUNK), bool))
    attn = jnp.where(causal, jnp.einsum('bhnid,bhnjd->bhnij', q_g, k_g), 0.0)
    o_intra = jnp.einsum('bhnij,bhnjv->bhniv', attn, v)
    chunk_decay = jnp.exp(cum_last[..., 0, :])

    def step(s, inp):
        qg, ke, vv, dec = inp
        o = jnp.einsum('bhcd,bhdv->bhcv', qg, s)
        s = s * dec[..., None] + jnp.einsum('bhcd,bhcv->bhdv', ke, vv)
        return s, o

    xs = tuple(jnp.moveaxis(t, 2, 0) for t in (q_g, k_end, v, chunk_decay))
    s_fin, o_inter = lax.scan(step, s0, xs)
    o = o_intra + jnp.moveaxis(o_inter, 0, 2)
    return o.reshape(b_, h_, l_, dv), s_fin


def gla_mixer(u_ctx, u_lat, w2, b, norm_w, need_ctx):
    nq = GLA_HEADS * GLA_DK

    def prep(u):
        q, k, v, og, lr = jnp.split(u, [nq, 2 * nq, 2 * nq + D_GROUP, 2 * nq + 2 * D_GROUP], axis=-1)
        q = to_heads(q, GLA_HEADS) * GLA_DK ** -0.5
        k = to_heads(k, GLA_HEADS)
        v = to_heads(v, GLA_HEADS)
        lr = lr.astype(F32).reshape(lr.shape[0], lr.shape[1], N_DIR, GLA_RANK)
        z = jnp.einsum('blsr,srk->blsk', lr, w2.astype(F32)) + b.astype(F32)
        log_alpha = jax.nn.log_sigmoid(z) / GLA_TAU
        g_f = to_heads(log_alpha[:, :, 0], GLA_HEADS)
        g_b = to_heads(log_alpha[:, :, 1], GLA_HEADS)
        return ((q, k, v, g_f), (q, k, v, g_b)), og

    ctx_in, og_c = prep(u_ctx)
    lat_in, og_l = prep(u_lat)
    s0 = jnp.zeros((u_lat.shape[0], GLA_HEADS, GLA_DK, GLA_DV), F32)
    o_c, o_l = bidirectional_scan(gla_scan, gla_scan, ctx_in, lat_in, s0, 2, need_ctx)

    def out(o, og):
        return (head_rmsnorm(from_heads(o), norm_w, GLA_HEADS) * jax.nn.silu(og.astype(F32))).astype(u_lat.dtype)

    return (out(o_c, og_c) if need_ctx else None), out(o_l, og_l)


def hyena_filters(l_, w1, b1, w2, b2, w3, b3, w4, freq):
    pos = jnp.arange(l_, dtype=F32)
    t = pos / max(l_ - 1, 1)
    bands = jnp.linspace(1e-4, HY_BANDS - 1, HY_BANDS, dtype=F32)
    ang = (2.0 * math.pi / l_) * pos[:, None] * bands[None, :]
    feats = jnp.concatenate([t[:, None], jnp.cos(ang), -jnp.sin(ang)], axis=-1)
    f = freq.astype(F32)
    h = jnp.sin(f[0] * (feats @ w1.astype(F32) + b1.astype(F32)))
    h = jnp.sin(f[1] * (h @ w2.astype(F32) + b2.astype(F32)))
    h = jnp.sin(f[2] * (h @ w3.astype(F32) + b3.astype(F32)))
    h = h @ w4.astype(F32)
    max_decay = math.log(HY_DECAY_TARGET) / HY_FAST_DECAY
    min_decay = math.log(HY_DECAY_TARGET) / HY_SLOW_DECAY
    deltas = jnp.abs(jnp.linspace(min_decay, max_decay, D_GROUP, dtype=F32))
    window = jnp.exp(-t[:, None] * deltas[None, :])
    return h.reshape(l_, HY_ORDER, N_DIR, D_GROUP) * window[:, None, None, :]


def long_conv(u, h_fwd, h_bwd, d_skip):
    l_, ch = h_fwd.shape
    kern = jnp.concatenate([h_fwd, jnp.zeros((1, ch), F32), jnp.flip(h_bwd[1:], 0)], axis=0)
    spec = jnp.fft.rfft(u, n=2 * l_, axis=1) * jnp.fft.rfft(kern, axis=0)[None]
    y = jnp.fft.irfft(spec, n=2 * l_, axis=1)[:, :l_]
    return y + u * d_skip


def hyena_mixer(u_ctx, u_lat, conv_w, conv_b, w1, b1, w2, b2, w3, b3, w4, freq, d_skip, need_ctx):
    d = d_skip.astype(F32)

    def run(u):
        l_ = u.shape[1]
        uc = short_conv(u, conv_w, conv_b).astype(F32)
        v, x1, x2 = jnp.split(uc, [D_GROUP, 2 * D_GROUP], axis=-1)
        h = hyena_filters(l_, w1, b1, w2, b2, w3, b3, w4, freq)
        z = x1 * long_conv(v, h[:, 0, 0], h[:, 0, 1], d[0])
        return (x2 * long_conv(z, h[:, 1, 0], h[:, 1, 1], d[1])).astype(u.dtype)

    return (run(u_ctx) if need_ctx else None), run(u_lat)


def ssd_scan(x, dt, bm, cm, s0, a):
    b_, l_, h_, p_ = x.shape
    g_, n_ = bm.shape[-2:]
    r_ = h_ // g_
    n = l_ // CHUNK
    cum = jnp.cumsum((dt * a).reshape(b_, n, CHUNK, g_, r_), axis=2)
    xdt = (x * dt[..., None]).reshape(b_, n, CHUNK, g_, r_, p_)
    bc = bm.reshape(b_, n, CHUNK, g_, n_)
    cc = cm.reshape(b_, n, CHUNK, g_, n_)
    cum_t = jnp.moveaxis(cum, 2, -1)
    causal = jnp.tril(jnp.ones((CHUNK, CHUNK), bool))
    seg = jnp.exp(jnp.where(causal, cum_t[..., :, None] - cum_t[..., None, :], -jnp.inf))
    cb = jnp.einsum('bnigs,bnjgs->bngij', cc, bc)
    y_diag = jnp.einsum('bngij,bngrij,bnjgrp->bnigrp', cb, seg, xdt)
    to_end = jnp.exp(cum[:, :, -1:] - cum)
    states = jnp.einsum('bncgs,bncgr,bncgrp->bngrsp', bc, to_end, xdt)
    chunk_decay = jnp.exp(cum[:, :, -1])
    c_dec = jnp.exp(cum)

    def step(s, inp):
        c_i, cd_i, st_i, dec_i = inp
        y_off = jnp.einsum('bcgs,bcgr,bgrsp->bcgrp', c_i, cd_i, s)
        s = s * dec_i[..., None, None] + st_i
        return s, y_off

    xs = tuple(jnp.moveaxis(t, 1, 0) for t in (cc, c_dec, states, chunk_decay))
    s_fin, y_off = lax.scan(step, s0, xs)
    y = y_diag + jnp.moveaxis(y_off, 0, 1)
    return y.reshape(b_, l_, h_, p_), s_fin


def mamba_mixer(u_ctx, u_lat, conv_w, conv_b, a_log, dt_bias, d_skip, norm_w, rows, need_ctx):
    nx = MB_HEADS * MB_HEADDIM
    nbc = MB_GROUPS * MB_STATE

    def prep(u):
        b_, l_, _ = u.shape
        z, xbc, dt_raw = jnp.split(u, [D_GROUP, D_GROUP + MB_CONV_CH], axis=-1)
        xbc = jax.nn.silu(short_conv(xbc, conv_w, conv_b)).astype(F32)
        xs, bm, cm = jnp.split(xbc, [nx, nx + nbc], axis=-1)
        xs = xs.reshape(b_, l_, MB_HEADS, MB_HEADDIM)
        bm = bm.reshape(b_, l_, MB_GROUPS, MB_STATE)
        cm = cm.reshape(b_, l_, MB_GROUPS, MB_STATE)
        dt = jax.nn.softplus(dt_raw.astype(F32).reshape(b_, l_, N_DIR, MB_HEADS) + dt_bias.astype(F32))
        return ((xs, dt[:, :, 0], bm, cm), (xs, dt[:, :, 1], bm, cm)), xs, z

    u_lat = to_col_major(u_lat, rows)
    ctx_in, x_c, z_c = prep(u_ctx)
    lat_in, x_l, z_l = prep(u_lat)
    a = -jnp.exp(a_log.astype(F32))
    s0 = jnp.zeros((u_lat.shape[0], MB_GROUPS, MB_HEADS // MB_GROUPS, MB_STATE, MB_HEADDIM), F32)
    o_c, o_l = bidirectional_scan(functools.partial(ssd_scan, a=a[0]), functools.partial(ssd_scan, a=a[1]),
                                  ctx_in, lat_in, s0, 1, need_ctx)
    d = d_skip.astype(F32)

    def out(o, xs, z):
        y = (o + xs * d[:, None]).reshape(o.shape[0], o.shape[1], D_GROUP)
        return head_rmsnorm(y * jax.nn.silu(z.astype(F32)), norm_w, MB_HEADS).astype(u_lat.dtype)

    y_l = to_row_major(out(o_l, x_l, z_l), rows)
    return (out(o_c, x_c, z_c) if need_ctx else None), y_l


def delta_scan(q, k, v, beta, g, s0):
    b_, h_, l_, dk = q.shape
    dv = v.shape[-1]
    n = l_ // CHUNK
    q, k, v = [t.reshape(b_, h_, n, CHUNK, t.shape[-1]) for t in (q, k, v)]
    beta = beta.reshape(b_, h_, n, CHUNK)
    cum = jnp.cumsum(g.reshape(b_, h_, n, CHUNK), axis=-1)
    incl = jnp.tril(jnp.ones((CHUNK, CHUNK), bool))
    strict = jnp.tril(jnp.ones((CHUNK, CHUNK), bool), -1)
    decay = jnp.exp(jnp.where(incl, cum[..., :, None] - cum[..., None, :], -jnp.inf))
    k_beta = k * beta[..., None]
    m = jnp.where(strict, jnp.einsum('bhnid,bhnjd->bhnij', k_beta, k) * decay, 0.0)
    rhs = jnp.concatenate([v * beta[..., None], k_beta * jnp.exp(cum)[..., None]], axis=-1)
    sol = lax.linalg.triangular_solve(m + jnp.eye(CHUNK, dtype=m.dtype), rhs, left_side=True, lower=True,
                                      unit_diagonal=True)
    u_val, w_key = sol[..., :dv], sol[..., dv:]
    attn = jnp.where(incl, jnp.einsum('bhnid,bhnjd->bhnij', q, k) * decay, 0.0)
    q_dec = q * jnp.exp(cum)[..., None]
    k_end = k * jnp.exp(cum[..., -1:] - cum)[..., None]
    chunk_decay = jnp.exp(cum[..., -1])

    def step(s, inp):
        qd, ke, uu, ww, aa, dc = inp
        v_new = uu - jnp.einsum('bhcd,bhdv->bhcv', ww, s)
        o = jnp.einsum('bhcd,bhdv->bhcv', qd, s) + jnp.einsum('bhcs,bhsv->bhcv', aa, v_new)
        s = s * dc[..., None, None] + jnp.einsum('bhcd,bhcv->bhdv', ke, v_new)
        return s, o

    xs = tuple(jnp.moveaxis(t, 2, 0) for t in (q_dec, k_end, u_val, w_key, attn, chunk_decay))
    s_fin, o = lax.scan(step, s0, xs)
    return jnp.moveaxis(o, 0, 2).reshape(b_, h_, l_, dv), s_fin


def deltanet_mixer(u_ctx, u_lat, conv_w, a_log, dt_bias, norm_w, need_ctx):
    nqk = DN_HEADS * DN_DK
    nv = DN_HEADS * DN_DV

    def prep(u):
        b_, l_, _ = u.shape
        qkv, og, beta_raw, a_raw = jnp.split(u, [2 * nqk + nv, 2 * nqk + 2 * nv, 2 * nqk + 2 * nv + N_DIR * DN_HEADS],
                                             axis=-1)
        qkv = jax.nn.silu(short_conv(qkv, conv_w))
        q, k, v = jnp.split(qkv, [nqk, 2 * nqk], axis=-1)
        q = l2norm(to_heads(q, DN_HEADS)) * DN_DK ** -0.5
        k = l2norm(to_heads(k, DN_HEADS))
        v = to_heads(v, DN_HEADS)
        beta = jax.nn.sigmoid(beta_raw.astype(F32)).reshape(b_, l_, N_DIR, DN_HEADS).transpose(2, 0, 3, 1)
        g = -jnp.exp(a_log.astype(F32)) * jax.nn.softplus(
            a_raw.astype(F32).reshape(b_, l_, N_DIR, DN_HEADS) + dt_bias.astype(F32))
        g = g.transpose(2, 0, 3, 1)
        return ((q, k, v, beta[0], g[0]), (q, k, v, beta[1], g[1])), og

    ctx_in, og_c = prep(u_ctx)
    lat_in, og_l = prep(u_lat)
    s0 = jnp.zeros((u_lat.shape[0], DN_HEADS, DN_DK, DN_DV), F32)
    o_c, o_l = bidirectional_scan(delta_scan, delta_scan, ctx_in, lat_in, s0, 2, need_ctx)

    def out(o, og):
        return (head_rmsnorm(from_heads(o), norm_w, DN_HEADS) * jax.nn.silu(og.astype(F32))).astype(u_lat.dtype)

    return (out(o_c, og_c) if need_ctx else None), out(o_l, og_l)


def ec_moe(h, router, w_gate, w_up, w_down):
    b_, t_, d_ = h.shape
    cap = EC_CAPACITY * t_ // N_EXPERTS
    aff = jax.nn.softmax(jnp.einsum('btd,de->bte', h, router).astype(F32), axis=-1)
    weight, idx = lax.top_k(jnp.swapaxes(aff, 1, 2), cap)
    xs = jax.vmap(lambda hb, ib: hb[ib])(h, idx)
    a = jnp.einsum('becd,edf->becf', xs, w_gate)
    u = jnp.einsum('becd,edf->becf', xs, w_up)
    y = jnp.einsum('becf,efd->becd', jax.nn.silu(a) * u, w_down) * weight[..., None].astype(h.dtype)
    scatter = lambda ib, yb: jnp.zeros((t_, d_), h.dtype).at[ib.reshape(-1)].add(yb.reshape(-1, d_))
    return jax.vmap(scatter)(idx, y)


def setup_inputs(seed: int = 0) -> dict:
    key = jax.random.key(seed)
    keys = iter(jax.random.split(key, 48))

    def nrm(shape, scale):
        return jax.random.normal(next(keys), shape, F32) * scale

    def gains(shape):
        return 1.0 + nrm(shape, 0.05)

    def a_log(shape):
        return jnp.log(jax.random.uniform(next(keys), shape, F32, 1.0, 16.0))

    def dt_bias(shape):
        dt = jnp.exp(jax.random.uniform(next(keys), shape, F32, math.log(1e-3), math.log(1e-1)))
        return dt + jnp.log(-jnp.expm1(-dt))

    d = D_MODEL
    hf = HY_FILTER_HIDDEN
    return {
        'x': nrm((BATCH, SEQ, d), 1.0),
        'c': nrm((BATCH, d), 1.0),
        'ctx': nrm((BATCH, CTX_LEN, d), 1.0),
        'c_ctx': nrm((d,), 1.0),
        'w_mod': nrm((DEPTH, d, N_MOD * d), 0.5 * d ** -0.5),
        'b_mod': nrm((DEPTH, N_MOD * d), 0.02),
        'norm_g': gains((DEPTH, 4, d)),
        'w_in': nrm((DEPTH, d, D_IN_PROJ), d ** -0.5),
        'w_out': nrm((DEPTH, D_MIX, d), D_MIX ** -0.5),
        'gla_w2': nrm((DEPTH, N_DIR, GLA_RANK, GLA_HEADS * GLA_DK), GLA_RANK ** -0.5),
        'gla_b': nrm((DEPTH, N_DIR, GLA_HEADS * GLA_DK), 0.1),
        'gla_norm': gains((DEPTH, D_GROUP)),
        'hy_conv_w': nrm((DEPTH, SHORT_CONV, HY_COLS), SHORT_CONV ** -0.5),
        'hy_conv_b': nrm((DEPTH, HY_COLS), 0.02),
        'hy_w1': nrm((DEPTH, HY_EMB, hf), HY_EMB ** -0.5),
        'hy_b1': nrm((DEPTH, hf), 0.1),
        'hy_w2': nrm((DEPTH, hf, hf), hf ** -0.5),
        'hy_b2': nrm((DEPTH, hf), 0.1),
        'hy_w3': nrm((DEPTH, hf, hf), hf ** -0.5),
        'hy_b3': nrm((DEPTH, hf), 0.1),
        'hy_w4': nrm((DEPTH, hf, HY_ORDER * N_DIR * D_GROUP), 0.1 * hf ** -0.5),
        'hy_freq': gains((DEPTH, 3, hf)),
        'hy_d': nrm((DEPTH, HY_ORDER, D_GROUP), 1.0),
        'mb_conv_w': nrm((DEPTH, SHORT_CONV, MB_CONV_CH), SHORT_CONV ** -0.5),
        'mb_conv_b': nrm((DEPTH, MB_CONV_CH), 0.02),
        'mb_a_log': a_log((DEPTH, N_DIR, MB_HEADS)),
        'mb_dt_bias': dt_bias((DEPTH, N_DIR, MB_HEADS)),
        'mb_d': gains((DEPTH, MB_HEADS)),
        'mb_norm': gains((DEPTH, D_GROUP)),
        'dn_conv_w': nrm((DEPTH, SHORT_CONV, 2 * DN_HEADS * DN_DK + DN_HEADS * DN_DV), SHORT_CONV ** -0.5),
        'dn_a_log': a_log((DEPTH, N_DIR, DN_HEADS)),
        'dn_dt_bias': dt_bias((DEPTH, N_DIR, DN_HEADS)),
        'dn_norm': gains((DEPTH, D_GROUP)),
        'moe_router': nrm((DEPTH, d, N_EXPERTS), d ** -0.5),
        'moe_w_gate': nrm((DEPTH, N_EXPERTS, d, D_EXPERT), d ** -0.5),
        'moe_w_up': nrm((DEPTH, N_EXPERTS, d, D_EXPERT), d ** -0.5),
        'moe_w_down': nrm((DEPTH, N_EXPERTS, D_EXPERT, d), D_EXPERT ** -0.5),
    }


def reference(x, c, ctx, c_ctx, w_mod, b_mod, norm_g, w_in, w_out, gla_w2, gla_b, gla_norm,
              hy_conv_w, hy_conv_b, hy_w1, hy_b1, hy_w2, hy_b2, hy_w3, hy_b3, hy_w4, hy_freq, hy_d,
              mb_conv_w, mb_conv_b, mb_a_log, mb_dt_bias, mb_d, mb_norm,
              dn_conv_w, dn_a_log, dn_dt_bias, dn_norm,
              moe_router, moe_w_gate, moe_w_up, moe_w_down):
    rows = x.shape[1] // GRID_W
    xc = ctx
    splits = [GLA_COLS, GLA_COLS + HY_COLS, GLA_COLS + HY_COLS + MB_COLS]
    for i in range(DEPTH):
        need_ctx = i < DEPTH - 1
        sh1, sc1, g1, sh2, sc2, g2 = [t[:, None, :] for t in modulation(c, w_mod[i], b_mod[i])]
        sh1c, sc1c, g1c, sh2c, sc2c, g2c = modulation(c_ctx, w_mod[i], b_mod[i])

        u_l = modulate(x, norm_g[i, 0], sh1, sc1) @ w_in[i]
        u_c = modulate(xc, norm_g[i, 0], sh1c, sc1c) @ w_in[i]
        ua_c, ub_c, uc_c, ud_c = jnp.split(u_c, splits, axis=-1)
        ua_l, ub_l, uc_l, ud_l = jnp.split(u_l, splits, axis=-1)
        ya_c, ya_l = gla_mixer(ua_c, ua_l, gla_w2[i], gla_b[i], gla_norm[i], need_ctx)
        yb_c, yb_l = hyena_mixer(ub_c, ub_l, hy_conv_w[i], hy_conv_b[i], hy_w1[i], hy_b1[i], hy_w2[i], hy_b2[i],
                                 hy_w3[i], hy_b3[i], hy_w4[i], hy_freq[i], hy_d[i], need_ctx)
        yc_c, yc_l = mamba_mixer(uc_c, uc_l, mb_conv_w[i], mb_conv_b[i], mb_a_log[i], mb_dt_bias[i], mb_d[i],
                                 mb_norm[i], rows, need_ctx)
        yd_c, yd_l = deltanet_mixer(ud_c, ud_l, dn_conv_w[i], dn_a_log[i], dn_dt_bias[i], dn_norm[i], need_ctx)
        y_l = jnp.concatenate([ya_l, yb_l, yc_l, yd_l], axis=-1) @ w_out[i]
        x = x + (g1 * rmsnorm(y_l, norm_g[i, 1]).astype(F32)).astype(x.dtype)

        h_l = modulate(x, norm_g[i, 2], sh2, sc2)
        m_l = ec_moe(h_l, moe_router[i], moe_w_gate[i], moe_w_up[i], moe_w_down[i])
        x = x + (g2 * rmsnorm(m_l, norm_g[i, 3]).astype(F32)).astype(x.dtype)

        if need_ctx:
            y_c = jnp.concatenate([ya_c, yb_c, yc_c, yd_c], axis=-1) @ w_out[i]
            xc = xc + (g1c * rmsnorm(y_c, norm_g[i, 1]).astype(F32)).astype(xc.dtype)
            h_c = modulate(xc, norm_g[i, 2], sh2c, sc2c)
            m_c = ec_moe(h_c, moe_router[i], moe_w_gate[i], moe_w_up[i], moe_w_down[i])
            xc = xc + (g2c * rmsnorm(m_c, norm_g[i, 3]).astype(F32)).astype(xc.dtype)
    return x
```

```python
import math, functools
import jax
import jax.numpy as jnp
from jax import lax
from jax.experimental import pallas as pl
from jax.experimental.pallas import tpu as pltpu

F32 = jnp.float32
BF16 = jnp.bfloat16

D_MODEL = 1024
DEPTH = 2
GRID_W = 64
N_MIXERS = 4
D_MIX = D_MODEL
D_GROUP = D_MIX // N_MIXERS
CHUNK = 64
SHORT_CONV = 3
N_DIR = 2
EPS = 1e-6
N_MOD = 6

GLA_HEADS = 4
GLA_DK = D_GROUP // (2 * GLA_HEADS)
GLA_DV = D_GROUP // GLA_HEADS
GLA_RANK = 16
GLA_TAU = 16.0

HY_ORDER = 2
HY_EMB = 33
HY_BANDS = (HY_EMB - 1) // 2
HY_FILTER_HIDDEN = 64
HY_DECAY_TARGET = 1e-2
HY_FAST_DECAY = 0.3
HY_SLOW_DECAY = 1.5

MB_HEADS = 4
MB_HEADDIM = D_GROUP // MB_HEADS
MB_STATE = 128
MB_GROUPS = 2
MB_CONV_CH = D_GROUP + 2 * MB_GROUPS * MB_STATE

DN_HEADS = 4
DN_DK = D_GROUP // DN_HEADS
DN_DV = D_GROUP // DN_HEADS

N_EXPERTS = 16
EC_CAPACITY = 2
D_EXPERT = 1024

GLA_COLS = 2 * GLA_HEADS * GLA_DK + 2 * D_GROUP + N_DIR * GLA_RANK
HY_COLS = (HY_ORDER + 1) * D_GROUP
MB_COLS = D_GROUP + MB_CONV_CH + N_DIR * MB_HEADS
DN_COLS = 4 * D_GROUP + 2 * N_DIR * DN_HEADS
D_IN_PROJ = GLA_COLS + HY_COLS + MB_COLS + DN_COLS

VMEM_LIMIT_BYTES = 48 * 1024 * 1024
HIGHEST = lax.Precision.HIGHEST


def _params(*sem):
    return pltpu.CompilerParams(dimension_semantics=sem, vmem_limit_bytes=VMEM_LIMIT_BYTES)


def _modulation_kernel(c_ref, w_ref, b_ref, o_ref):
    c = c_ref[...]
    s = c * jax.nn.sigmoid(c)
    o_ref[...] = jnp.dot(s, w_ref[...], preferred_element_type=F32, precision=HIGHEST) + b_ref[...]


def modulation_all(cond, w_mod, b_mod):
    r, d = cond.shape
    n = w_mod.shape[1]
    tn = 1024
    return pl.pallas_call(
        _modulation_kernel,
        grid=(n // tn,),
        in_specs=[pl.BlockSpec((r, d), lambda j: (0, 0)),
                  pl.BlockSpec((d, tn), lambda j: (0, j)),
                  pl.BlockSpec((1, tn), lambda j: (0, j))],
        out_specs=pl.BlockSpec((r, tn), lambda j: (0, j)),
        out_shape=jax.ShapeDtypeStruct((r, n), F32),
        compiler_params=_params("arbitrary"),
        name="modulation",
    )(cond, w_mod, b_mod.reshape(1, n))


def _norm_mod(x, g, sh, sc):
    ms = jnp.mean(x * x, axis=-1, keepdims=True)
    y = x * lax.rsqrt(ms + EPS) * g
    return y * (1.0 + sc) + sh


def _inproj_kernel(x_ref, sh_ref, sc_ref, g_ref, w_ref, o_ref, *, tn):
    yb = _norm_mod(x_ref[...], g_ref[...], sh_ref[0], sc_ref[0]).astype(BF16)
    n = o_ref.shape[1]
    for j in range(0, n, tn):
        w = min(tn, n - j)
        o_ref[:, j:j + w] = jnp.dot(yb, w_ref[:, j:j + w], preferred_element_type=F32)


def norm_mod_proj(x, g, shift, scale, w_bf16, tm=256, tn=512):
    nb, l_, d = x.shape
    n = w_bf16.shape[1]
    tm = min(tm, l_)
    per_b = l_ // tm
    nmod = shift.shape[0]
    mod_map = (lambda i: (i // per_b, 0, 0)) if nmod > 1 else (lambda i: (0, 0, 0))
    out = pl.pallas_call(
        functools.partial(_inproj_kernel, tn=tn),
        grid=(nb * per_b,),
        in_specs=[pl.BlockSpec((tm, d), lambda i: (i, 0)),
                  pl.BlockSpec((1, 1, d), mod_map),
                  pl.BlockSpec((1, 1, d), mod_map),
                  pl.BlockSpec((1, d), lambda i: (0, 0)),
                  pl.BlockSpec((d, n), lambda i: (0, 0))],
        out_specs=pl.BlockSpec((tm, n), lambda i: (i, 0)),
        out_shape=jax.ShapeDtypeStruct((nb * l_, n), F32),
        compiler_params=_params("arbitrary"),
        name="norm_mod_proj",
    )(x.reshape(nb * l_, d), shift.reshape(nmod, 1, d), scale.reshape(nmod, 1, d), g.reshape(1, d), w_bf16)
    return out.reshape(nb, l_, n)


def _outproj_kernel(y_ref, x_ref, gate_ref, g_ref, w_ref, o_ref):
    p = jnp.dot(y_ref[...].astype(BF16), w_ref[...], preferred_element_type=F32)
    ms = jnp.mean(p * p, axis=-1, keepdims=True)
    o_ref[...] = x_ref[...] + gate_ref[0] * (p * lax.rsqrt(ms + EPS) * g_ref[...])


def proj_norm_residual(y, x, gate, g, w_bf16, tm=512):
    nb, l_, d = x.shape
    k = y.shape[-1]
    tm = min(tm, l_)
    per_b = l_ // tm
    nmod = gate.shape[0]
    mod_map = (lambda i: (i // per_b, 0, 0)) if nmod > 1 else (lambda i: (0, 0, 0))
    out = pl.pallas_call(
        _outproj_kernel,
        grid=(nb * per_b,),
        in_specs=[pl.BlockSpec((tm, k), lambda i: (i, 0)),
                  pl.BlockSpec((tm, d), lambda i: (i, 0)),
                  pl.BlockSpec((1, 1, d), mod_map),
                  pl.BlockSpec((1, d), lambda i: (0, 0)),
                  pl.BlockSpec((k, d), lambda i: (0, 0))],
        out_specs=pl.BlockSpec((tm, d), lambda i: (i, 0)),
        out_shape=jax.ShapeDtypeStruct((nb * l_, d), F32),
        compiler_params=_params("arbitrary"),
        name="proj_norm_residual",
    )(y.reshape(nb * l_, k), x.reshape(nb * l_, d), gate.reshape(nmod, 1, d), g.reshape(1, d), w_bf16)
    return out.reshape(nb, l_, d)


def _router_kernel(x_ref, sh_ref, sc_ref, g_ref, r_ref, h_ref, aff_ref):
    h = _norm_mod(x_ref[...], g_ref[...], sh_ref[0], sc_ref[0])
    h_ref[...] = h.astype(h_ref.dtype)
    logits = jnp.dot(h, r_ref[...], preferred_element_type=F32, precision=HIGHEST)
    m = jnp.max(logits, axis=-1, keepdims=True)
    e = jnp.exp(logits - m)
    aff_ref[...] = e / jnp.sum(e, axis=-1, keepdims=True)


def norm_mod_router(x, g, shift, scale, router, tm=512):
    nb, l_, d = x.shape
    e = router.shape[1]
    tm = min(tm, l_)
    per_b = l_ // tm
    nmod = shift.shape[0]
    mod_map = (lambda i: (i // per_b, 0, 0)) if nmod > 1 else (lambda i: (0, 0, 0))
    h, aff = pl.pallas_call(
        _router_kernel,
        grid=(nb * per_b,),
        in_specs=[pl.BlockSpec((tm, d), lambda i: (i, 0)),
                  pl.BlockSpec((1, 1, d), mod_map),
                  pl.BlockSpec((1, 1, d), mod_map),
                  pl.BlockSpec((1, d), lambda i: (0, 0)),
                  pl.BlockSpec((d, e), lambda i: (0, 0))],
        out_specs=[pl.BlockSpec((tm, d), lambda i: (i, 0)),
                   pl.BlockSpec((tm, e), lambda i: (i, 0))],
        out_shape=[jax.ShapeDtypeStruct((nb * l_, d), BF16),
                   jax.ShapeDtypeStruct((nb * l_, e), F32)],
        compiler_params=_params("arbitrary"),
        name="norm_mod_router",
    )(x.reshape(nb * l_, d), shift.reshape(nmod, 1, d), scale.reshape(nmod, 1, d), g.reshape(1, d), router)
    return h.reshape(nb, l_, d), aff.reshape(nb, l_, e)


def _ffn_kernel(x_ref, wg_ref, wu_ref, wd_ref, o_ref):
    x = x_ref[0]
    a = jnp.dot(x, wg_ref[0], preferred_element_type=F32)
    u = jnp.dot(x, wu_ref[0], preferred_element_type=F32)
    hid = (a * jax.nn.sigmoid(a) * u).astype(BF16)
    o_ref[0] = jnp.dot(hid, wd_ref[0], preferred_element_type=F32)


def expert_ffn(xs, wg, wu, wd, tm=512):
    e, r, d = xs.shape
    f = wg.shape[2]
    tm = min(tm, r)
    return pl.pallas_call(
        _ffn_kernel,
        grid=(e, r // tm),
        in_specs=[pl.BlockSpec((1, tm, d), lambda i, j: (i, j, 0)),
                  pl.BlockSpec((1, d, f), lambda i, j: (i, 0, 0)),
                  pl.BlockSpec((1, d, f), lambda i, j: (i, 0, 0)),
                  pl.BlockSpec((1, f, d), lambda i, j: (i, 0, 0))],
        out_specs=pl.BlockSpec((1, tm, d), lambda i, j: (i, j, 0)),
        out_shape=jax.ShapeDtypeStruct((e, r, d), F32),
        compiler_params=_params("arbitrary", "arbitrary"),
        name="expert_ffn",
    )(xs, wg, wu, wd)


def head_rmsnorm(o, w, n_heads):
    b_, l_, hd = o.shape
    oh = o.reshape(b_, l_, n_heads, hd // n_heads)
    oh = oh * lax.rsqrt(jnp.mean(oh * oh, axis=-1, keepdims=True) + EPS)
    return oh.reshape(b_, l_, hd) * w.astype(F32)


def l2norm(t):
    return t * lax.rsqrt(jnp.sum(t * t, axis=-1, keepdims=True) + EPS)


def to_heads(t, n_heads):
    b_, l_, hd = t.shape
    return t.reshape(b_, l_, n_heads, hd // n_heads).transpose(0, 2, 1, 3).astype(F32)


def from_heads(t):
    b_, h_, l_, d_ = t.shape
    return t.transpose(0, 2, 1, 3).reshape(b_, l_, h_ * d_)


def to_col_major(u, rows):
    b_, l_, ch = u.shape
    return u.reshape(b_, rows, GRID_W, ch).transpose(0, 2, 1, 3).reshape(b_, l_, ch)


def to_row_major(u, rows):
    b_, l_, ch = u.shape
    return u.reshape(b_, GRID_W, rows, ch).transpose(0, 2, 1, 3).reshape(b_, l_, ch)


def short_conv(u, w, b=None):
    k_, ch = w.shape
    y = lax.conv_general_dilated(u, w[:, None, :].astype(u.dtype), window_strides=(1,),
                                 padding=[(k_ // 2, k_ // 2)], dimension_numbers=('NWC', 'WIO', 'NWC'),
                                 feature_group_count=ch)
    if b is not None:
        y = y + b.astype(u.dtype)
    return y


def bidirectional_scan(scan_f, scan_b, ctx_in, lat_in, s0, t_axis, need_ctx):
    rev = lambda ts: tuple(jnp.flip(t, t_axis) for t in ts)
    oc_f, s_f = scan_f(*ctx_in[0], s0)
    oc_b, s_b = scan_b(*rev(ctx_in[1]), s0)
    ol_f, _ = scan_f(*lat_in[0], s_f)
    ol_b, _ = scan_b(*rev(lat_in[1]), s_b)
    o_lat = ol_f + jnp.flip(ol_b, t_axis)
    o_ctx = (oc_f + jnp.flip(oc_b, t_axis)) if need_ctx else None
    return o_ctx, o_lat


def gla_scan(q, k, v, g, s0):
    b_, h_, l_, dk = q.shape
    dv = v.shape[-1]
    n = l_ // CHUNK
    q, k, v, g = [t.reshape(b_, h_, n, CHUNK, t.shape[-1]) for t in (q, k, v, g)]
    cum = jnp.cumsum(g, axis=-2)
    cum_last = cum[..., -1:, :]
    q_g = q * jnp.exp(cum)
    k_g = k * jnp.exp(-cum)
    k_end = k * jnp.exp(cum_last - cum)
    causal = jnp.tril(jnp.ones((CHUNK, CHUNK), bool))
    attn = jnp.where(causal, jnp.einsum('bhnid,bhnjd->bhnij', q_g, k_g), 0.0)
    o_intra = jnp.einsum('bhnij,bhnjv->bhniv', attn, v)
    chunk_decay = jnp.exp(cum_last[..., 0, :])

    def step(s, inp):
        qg, ke, vv, dec = inp
        o = jnp.einsum('bhcd,bhdv->bhcv', qg, s)
        s = s * dec[..., None] + jnp.einsum('bhcd,bhcv->bhdv', ke, vv)
        return s, o

    xs = tuple(jnp.moveaxis(t, 2, 0) for t in (q_g, k_end, v, chunk_decay))
    s_fin, o_inter = lax.scan(step, s0, xs)
    o = o_intra + jnp.moveaxis(o_inter, 0, 2)
    return o.reshape(b_, h_, l_, dv), s_fin


def gla_mixer(u_ctx, u_lat, w2, b, norm_w, need_ctx):
    nq = GLA_HEADS * GLA_DK

    def prep(u):
        q, k, v, og, lr = jnp.split(u, [nq, 2 * nq, 2 * nq + D_GROUP, 2 * nq + 2 * D_GROUP], axis=-1)
        q = to_heads(q, GLA_HEADS) * GLA_DK ** -0.5
        k = to_heads(k, GLA_HEADS)
        v = to_heads(v, GLA_HEADS)
        lr = lr.astype(F32).reshape(lr.shape[0], lr.shape[1], N_DIR, GLA_RANK)
        z = jnp.einsum('blsr,srk->blsk', lr, w2.astype(F32)) + b.astype(F32)
        log_alpha = jax.nn.log_sigmoid(z) / GLA_TAU
        g_f = to_heads(log_alpha[:, :, 0], GLA_HEADS)
        g_b = to_heads(log_alpha[:, :, 1], GLA_HEADS)
        return ((q, k, v, g_f), (q, k, v, g_b)), og

    ctx_in, og_c = prep(u_ctx)
    lat_in, og_l = prep(u_lat)
    s0 = jnp.zeros((u_lat.shape[0], GLA_HEADS, GLA_DK, GLA_DV), F32)
    o_c, o_l = bidirectional_scan(gla_scan, gla_scan, ctx_in, lat_in, s0, 2, need_ctx)

    def out(o, og):
        return (head_rmsnorm(from_heads(o), norm_w, GLA_HEADS) * jax.nn.silu(og.astype(F32))).astype(u_lat.dtype)

    return (out(o_c, og_c) if need_ctx else None), out(o_l, og_l)


def hyena_filters(l_, w1, b1, w2, b2, w3, b3, w4, freq):
    pos = jnp.arange(l_, dtype=F32)
    t = pos / max(l_ - 1, 1)
    bands = jnp.linspace(1e-4, HY_BANDS - 1, HY_BANDS, dtype=F32)
    ang = (2.0 * math.pi / l_) * pos[:, None] * bands[None, :]
    feats = jnp.concatenate([t[:, None], jnp.cos(ang), -jnp.sin(ang)], axis=-1)
    f = freq.astype(F32)
    h = jnp.sin(f[0] * (feats @ w1.astype(F32) + b1.astype(F32)))
    h = jnp.sin(f[1] * (h @ w2.astype(F32) + b2.astype(F32)))
    h = jnp.sin(f[2] * (h @ w3.astype(F32) + b3.astype(F32)))
    h = h @ w4.astype(F32)
    max_decay = math.log(HY_DECAY_TARGET) / HY_FAST_DECAY
    min_decay = math.log(HY_DECAY_TARGET) / HY_SLOW_DECAY
    deltas = jnp.abs(jnp.linspace(min_decay, max_decay, D_GROUP, dtype=F32))
    window = jnp.exp(-t[:, None] * deltas[None, :])
    return h.reshape(l_, HY_ORDER, N_DIR, D_GROUP) * window[:, None, None, :]


def long_conv(u, h_fwd, h_bwd, d_skip):
    l_, ch = h_fwd.shape
    kern = jnp.concatenate([h_fwd, jnp.zeros((1, ch), F32), jnp.flip(h_bwd[1:], 0)], axis=0)
    spec = jnp.fft.rfft(u, n=2 * l_, axis=1) * jnp.fft.rfft(kern, axis=0)[None]
    y = jnp.fft.irfft(spec, n=2 * l_, axis=1)[:, :l_]
    return y + u * d_skip


def hyena_mixer(u_ctx, u_lat, conv_w, conv_b, w1, b1, w2, b2, w3, b3, w4, freq, d_skip, need_ctx):
    d = d_skip.astype(F32)

    def run(u):
        l_ = u.shape[1]
        uc = short_conv(u, conv_w, conv_b).astype(F32)
        v, x1, x2 = jnp.split(uc, [D_GROUP, 2 * D_GROUP], axis=-1)
        h = hyena_filters(l_, w1, b1, w2, b2, w3, b3, w4, freq)
        z = x1 * long_conv(v, h[:, 0, 0], h[:, 0, 1], d[0])
        return (x2 * long_conv(z, h[:, 1, 0], h[:, 1, 1], d[1])).astype(u.dtype)

    return (run(u_ctx) if need_ctx else None), run(u_lat)


def ssd_scan(x, dt, bm, cm, s0, a):
    b_, l_, h_, p_ = x.shape
    g_, n_ = bm.shape[-2:]
    r_ = h_ // g_
    n = l_ // CHUNK
    cum = jnp.cumsum((dt * a).reshape(b_, n, CHUNK, g_, r_), axis=2)
    xdt = (x * dt[..., None]).reshape(b_, n, CHUNK, g_, r_, p_)
    bc = bm.reshape(b_, n, CHUNK, g_, n_)
    cc = cm.reshape(b_, n, CHUNK, g_, n_)
    cum_t = jnp.moveaxis(cum, 2, -1)
    causal = jnp.tril(jnp.ones((CHUNK, CHUNK), bool))
    seg = jnp.exp(jnp.where(causal, cum_t[..., :, None] - cum_t[..., None, :], -jnp.inf))
    cb = jnp.einsum('bnigs,bnjgs->bngij', cc, bc)
    y_diag = jnp.einsum('bngij,bngrij,bnjgrp->bnigrp', cb, seg, xdt)
    to_end = jnp.exp(cum[:, :, -1:] - cum)
    states = jnp.einsum('bncgs,bncgr,bncgrp->bngrsp', bc, to_end, xdt)
    chunk_decay = jnp.exp(cum[:, :, -1])
    c_dec = jnp.exp(cum)

    def step(s, inp):
        c_i, cd_i, st_i, dec_i = inp
        y_off = jnp.einsum('bcgs,bcgr,bgrsp->bcgrp', c_i, cd_i, s)
        s = s * dec_i[..., None, None] + st_i
        return s, y_off

    xs = tuple(jnp.moveaxis(t, 1, 0) for t in (cc, c_dec, states, chunk_decay))
    s_fin, y_off = lax.scan(step, s0, xs)
    y = y_diag + jnp.moveaxis(y_off, 0, 1)
    return y.reshape(b_, l_, h_, p_), s_fin


def mamba_mixer(u_ctx, u_lat, conv_w, conv_b, a_log, dt_bias, d_skip, norm_w, rows, need_ctx):
    nx = MB_HEADS * MB_HEADDIM
    nbc = MB_GROUPS * MB_STATE

    def prep(u):
        b_, l_, _ = u.shape
        z, xbc, dt_raw = jnp.split(u, [D_GROUP, D_GROUP + MB_CONV_CH], axis=-1)
        xbc = jax.nn.silu(short_conv(xbc, conv_w, conv_b)).astype(F32)
        xs, bm, cm = jnp.split(xbc, [nx, nx + nbc], axis=-1)
        xs = xs.reshape(b_, l_, MB_HEADS, MB_HEADDIM)
        bm = bm.reshape(b_, l_, MB_GROUPS, MB_STATE)
        cm = cm.reshape(b_, l_, MB_GROUPS, MB_STATE)
        dt = jax.nn.softplus(dt_raw.astype(F32).reshape(b_, l_, N_DIR, MB_HEADS) + dt_bias.astype(F32))
        return ((xs, dt[:, :, 0], bm, cm), (xs, dt[:, :, 1], bm, cm)), xs, z

    u_lat = to_col_major(u_lat, rows)
    ctx_in, x_c, z_c = prep(u_ctx)
    lat_in, x_l, z_l = prep(u_lat)
    a = -jnp.exp(a_log.astype(F32))
    s0 = jnp.zeros((u_lat.shape[0], MB_GROUPS, MB_HEADS // MB_GROUPS, MB_STATE, MB_HEADDIM), F32)
    o_c, o_l = bidirectional_scan(functools.partial(ssd_scan, a=a[0]), functools.partial(ssd_scan, a=a[1]),
                                  ctx_in, lat_in, s0, 1, need_ctx)
    d = d_skip.astype(F32)

    def out(o, xs, z):
        y = (o + xs * d[:, None]).reshape(o.shape[0], o.shape[1], D_GROUP)
        return head_rmsnorm(y * jax.nn.silu(z.astype(F32)), norm_w, MB_HEADS).astype(u_lat.dtype)

    y_l = to_row_major(out(o_l, x_l, z_l), rows)
    return (out(o_c, x_c, z_c) if need_ctx else None), y_l


def delta_scan(q, k, v, beta, g, s0):
    b_, h_, l_, dk = q.shape
    dv = v.shape[-1]
    n = l_ // CHUNK
    q, k, v = [t.reshape(b_, h_, n, CHUNK, t.shape[-1]) for t in (q, k, v)]
    beta = beta.reshape(b_, h_, n, CHUNK)
    cum = jnp.cumsum(g.reshape(b_, h_, n, CHUNK), axis=-1)
    incl = jnp.tril(jnp.ones((CHUNK, CHUNK), bool))
    strict = jnp.tril(jnp.ones((CHUNK, CHUNK), bool), -1)
    decay = jnp.exp(jnp.where(incl, cum[..., :, None] - cum[..., None, :], -jnp.inf))
    k_beta = k * beta[..., None]
    m = jnp.where(strict, jnp.einsum('bhnid,bhnjd->bhnij', k_beta, k) * decay, 0.0)
    rhs = jnp.concatenate([v * beta[..., None], k_beta * jnp.exp(cum)[..., None]], axis=-1)
    sol = lax.linalg.triangular_solve(m + jnp.eye(CHUNK, dtype=m.dtype), rhs, left_side=True, lower=True,
                                      unit_diagonal=True)
    u_val, w_key = sol[..., :dv], sol[..., dv:]
    attn = jnp.where(incl, jnp.einsum('bhnid,bhnjd->bhnij', q, k) * decay, 0.0)
    q_dec = q * jnp.exp(cum)[..., None]
    k_end = k * jnp.exp(cum[..., -1:] - cum)[..., None]
    chunk_decay = jnp.exp(cum[..., -1])

    def step(s, inp):
        qd, ke, uu, ww, aa, dc = inp
        v_new = uu - jnp.einsum('bhcd,bhdv->bhcv', ww, s)
        o = jnp.einsum('bhcd,bhdv->bhcv', qd, s) + jnp.einsum('bhcs,bhsv->bhcv', aa, v_new)
        s = s * dc[..., None, None] + jnp.einsum('bhcd,bhcv->bhdv', ke, v_new)
        return s, o

    xs = tuple(jnp.moveaxis(t, 2, 0) for t in (q_dec, k_end, u_val, w_key, attn, chunk_decay))
    s_fin, o = lax.scan(step, s0, xs)
    return jnp.moveaxis(o, 0, 2).reshape(b_, h_, l_, dv), s_fin


def deltanet_mixer(u_ctx, u_lat, conv_w, a_log, dt_bias, norm_w, need_ctx):
    nqk = DN_HEADS * DN_DK
    nv = DN_HEADS * DN_DV

    def prep(u):
        b_, l_, _ = u.shape
        qkv, og, beta_raw, a_raw = jnp.split(u, [2 * nqk + nv, 2 * nqk + 2 * nv, 2 * nqk + 2 * nv + N_DIR * DN_HEADS],
                                             axis=-1)
        qkv = jax.nn.silu(short_conv(qkv, conv_w))
        q, k, v = jnp.split(qkv, [nqk, 2 * nqk], axis=-1)
        q = l2norm(to_heads(q, DN_HEADS)) * DN_DK ** -0.5
        k = l2norm(to_heads(k, DN_HEADS))
        v = to_heads(v, DN_HEADS)
        beta = jax.nn.sigmoid(beta_raw.astype(F32)).reshape(b_, l_, N_DIR, DN_HEADS).transpose(2, 0, 3, 1)
        g = -jnp.exp(a_log.astype(F32)) * jax.nn.softplus(
            a_raw.astype(F32).reshape(b_, l_, N_DIR, DN_HEADS) + dt_bias.astype(F32))
        g = g.transpose(2, 0, 3, 1)
        return ((q, k, v, beta[0], g[0]), (q, k, v, beta[1], g[1])), og

    ctx_in, og_c = prep(u_ctx)
    lat_in, og_l = prep(u_lat)
    s0 = jnp.zeros((u_lat.shape[0], DN_HEADS, DN_DK, DN_DV), F32)
    o_c, o_l = bidirectional_scan(delta_scan, delta_scan, ctx_in, lat_in, s0, 2, need_ctx)

    def out(o, og):
        return (head_rmsnorm(from_heads(o), norm_w, DN_HEADS) * jax.nn.silu(og.astype(F32))).astype(u_lat.dtype)

    return (out(o_c, og_c) if need_ctx else None), out(o_l, og_l)


def ec_moe(h_bf16, aff, wg, wu, wd):
    b_, t_, d_ = h_bf16.shape
    cap = EC_CAPACITY * t_ // N_EXPERTS
    weight, idx = lax.top_k(jnp.swapaxes(aff, 1, 2), cap)
    xs = jax.vmap(lambda hb, ib: hb[ib])(h_bf16, idx)
    xs = xs.transpose(1, 0, 2, 3).reshape(N_EXPERTS, b_ * cap, d_)
    y = expert_ffn(xs, wg, wu, wd)
    y = y.reshape(N_EXPERTS, b_, cap, d_).transpose(1, 0, 2, 3) * weight[..., None]
    scatter = lambda ib, yb: jnp.zeros((t_, d_), F32).at[ib.reshape(-1)].add(yb.reshape(-1, d_))
    return jax.vmap(scatter)(idx, y)


def _rmsnorm_residual(x, m, gate, g):
    ms = jnp.mean(m * m, axis=-1, keepdims=True)
    return x + gate[:, None, :] * (m * lax.rsqrt(ms + EPS) * g)


def kernel(x, c, ctx, c_ctx, w_mod, b_mod, norm_g, w_in, w_out, gla_w2, gla_b, gla_norm,
           hy_conv_w, hy_conv_b, hy_w1, hy_b1, hy_w2, hy_b2, hy_w3, hy_b3, hy_w4, hy_freq, hy_d,
           mb_conv_w, mb_conv_b, mb_a_log, mb_dt_bias, mb_d, mb_norm,
           dn_conv_w, dn_a_log, dn_dt_bias, dn_norm,
           moe_router, moe_w_gate, moe_w_up, moe_w_down):
    b_ = x.shape[0]
    rows = x.shape[1] // GRID_W
    xc = ctx
    splits = [GLA_COLS, GLA_COLS + HY_COLS, GLA_COLS + HY_COLS + MB_COLS]
    cond = jnp.concatenate([c, c_ctx[None, :], jnp.zeros((8 - b_ - 1, D_MODEL), F32)], axis=0)
    for i in range(DEPTH):
        need_ctx = i < DEPTH - 1
        mod = modulation_all(cond, w_mod[i], b_mod[i])
        sh1, sc1, g1, sh2, sc2, g2 = jnp.split(mod[:b_], N_MOD, axis=-1)
        sh1c, sc1c, g1c, sh2c, sc2c, g2c = jnp.split(mod[b_:b_ + 1], N_MOD, axis=-1)
        w_in_b = w_in[i].astype(BF16)
        w_out_b = w_out[i].astype(BF16)

        u_l = norm_mod_proj(x, norm_g[i, 0], sh1, sc1, w_in_b)
        u_c = norm_mod_proj(xc, norm_g[i, 0], sh1c, sc1c, w_in_b)
        ua_c, ub_c, uc_c, ud_c = jnp.split(u_c, splits, axis=-1)
        ua_l, ub_l, uc_l, ud_l = jnp.split(u_l, splits, axis=-1)
        ya_c, ya_l = gla_mixer(ua_c, ua_l, gla_w2[i], gla_b[i], gla_norm[i], need_ctx)
        yb_c, yb_l = hyena_mixer(ub_c, ub_l, hy_conv_w[i], hy_conv_b[i], hy_w1[i], hy_b1[i], hy_w2[i], hy_b2[i],
                                 hy_w3[i], hy_b3[i], hy_w4[i], hy_freq[i], hy_d[i], need_ctx)
        yc_c, yc_l = mamba_mixer(uc_c, uc_l, mb_conv_w[i], mb_conv_b[i], mb_a_log[i], mb_dt_bias[i], mb_d[i],
                                 mb_norm[i], rows, need_ctx)
        yd_c, yd_l = deltanet_mixer(ud_c, ud_l, dn_conv_w[i], dn_a_log[i], dn_dt_bias[i], dn_norm[i], need_ctx)
        y_l = jnp.concatenate([ya_l, yb_l, yc_l, yd_l], axis=-1)
        x = proj_norm_residual(y_l, x, g1, norm_g[i, 1], w_out_b)

        wg = moe_w_gate[i].astype(BF16)
        wu = moe_w_up[i].astype(BF16)
        wd = moe_w_down[i].astype(BF16)
        h_l, aff_l = norm_mod_router(x, norm_g[i, 2], sh2, sc2, moe_router[i])
        m_l = ec_moe(h_l, aff_l, wg, wu, wd)
        x = _rmsnorm_residual(x, m_l, g2, norm_g[i, 3])

        if need_ctx:
            y_c = jnp.concatenate([ya_c, yb_c, yc_c, yd_c], axis=-1)
            xc = proj_norm_residual(y_c, xc, g1c, norm_g[i, 1], w_out_b)
            h_c, aff_c = norm_mod_router(xc, norm_g[i, 2], sh2c, sc2c, moe_router[i])
            m_c = ec_moe(h_c, aff_c, wg, wu, wd)
            xc = _rmsnorm_residual(xc, m_c, jnp.broadcast_to(g2c, (b_, D_MODEL)), norm_g[i, 3])
    return x
```

```python
import math, functools
import jax
import jax.numpy as jnp
from jax import lax
from jax.experimental import pallas as pl
from jax.experimental.pallas import tpu as pltpu

F32 = jnp.float32
BF16 = jnp.bfloat16

D_MODEL = 1024
DEPTH = 2
GRID_W = 64
N_MIXERS = 4
D_MIX = D_MODEL
D_GROUP = D_MIX // N_MIXERS
CHUNK = 64
SHORT_CONV = 3
N_DIR = 2
EPS = 1e-6
N_MOD = 6

GLA_HEADS = 4
GLA_DK = D_GROUP // (2 * GLA_HEADS)
GLA_DV = D_GROUP // GLA_HEADS
GLA_RANK = 16
GLA_TAU = 16.0

HY_ORDER = 2
HY_EMB = 33
HY_BANDS = (HY_EMB - 1) // 2
HY_FILTER_HIDDEN = 64
HY_DECAY_TARGET = 1e-2
HY_FAST_DECAY = 0.3
HY_SLOW_DECAY = 1.5

MB_HEADS = 4
MB_HEADDIM = D_GROUP // MB_HEADS
MB_STATE = 128
MB_GROUPS = 2
MB_CONV_CH = D_GROUP + 2 * MB_GROUPS * MB_STATE

DN_HEADS = 4
DN_DK = D_GROUP // DN_HEADS
DN_DV = D_GROUP // DN_HEADS
DN_QKV = 2 * DN_HEADS * DN_DK + DN_HEADS * DN_DV

N_EXPERTS = 16
EC_CAPACITY = 2
D_EXPERT = 1024

GLA_COLS = 2 * GLA_HEADS * GLA_DK + 2 * D_GROUP + N_DIR * GLA_RANK
HY_COLS = (HY_ORDER + 1) * D_GROUP
MB_COLS = D_GROUP + MB_CONV_CH + N_DIR * MB_HEADS
DN_COLS = 4 * D_GROUP + 2 * N_DIR * DN_HEADS
D_IN_PROJ = GLA_COLS + HY_COLS + MB_COLS + DN_COLS

LANES = 128
HALO = 8


def _lane_pad(n):
    return -(-n // LANES) * LANES


MIX_COLS = (GLA_COLS, HY_COLS, MB_COLS, DN_COLS)
MIX_W = tuple(_lane_pad(n) for n in MIX_COLS)
GLA_W, HY_W, MB_W, DN_W = MIX_W

VMEM_LIMIT_BYTES = 48 * 1024 * 1024
HIGHEST = lax.Precision.HIGHEST

NN = (((1,), (0,)), ((), ()))
NT = (((1,), (1,)), ((), ()))
TN = (((0,), (0,)), ((), ()))


def _params(*sem):
    return pltpu.CompilerParams(dimension_semantics=sem, vmem_limit_bytes=VMEM_LIMIT_BYTES)


def _dot(a, b, dims=NN, prec=None):
    return lax.dot_general(a, b, dims, precision=prec, preferred_element_type=F32)


def _iota(shape, axis):
    return lax.broadcasted_iota(jnp.int32, shape, axis)


def _silu(x):
    return x * jax.nn.sigmoid(x)


def _full(*shape):
    return pl.BlockSpec(shape, lambda *_: (0,) * len(shape))


def _modulation_kernel(c_ref, w_ref, b_ref, o_ref):
    o_ref[...] = _dot(_silu(c_ref[...]), w_ref[...], prec=HIGHEST) + b_ref[...]


def modulation_all(cond, w_mod, b_mod):
    r, d = cond.shape
    n = w_mod.shape[1]
    tn = 1024
    return pl.pallas_call(
        _modulation_kernel,
        grid=(n // tn,),
        in_specs=[pl.BlockSpec((r, d), lambda j: (0, 0)),
                  pl.BlockSpec((d, tn), lambda j: (0, j)),
                  pl.BlockSpec((1, tn), lambda j: (0, j))],
        out_specs=pl.BlockSpec((r, tn), lambda j: (0, j)),
        out_shape=jax.ShapeDtypeStruct((r, n), F32),
        compiler_params=_params("arbitrary"),
        name="modulation",
    )(cond, w_mod, b_mod.reshape(1, n))


def _norm_mod(x, g, sh, sc):
    ms = jnp.mean(x * x, axis=-1, keepdims=True)
    y = x * lax.rsqrt(ms + EPS) * g
    return y * (1.0 + sc) + sh


def _mod_map(per_b, ctx_tiles, nb):
    return lambda i: (jnp.where(i % per_b < ctx_tiles, nb, i // per_b), 0, 0)


def _inproj_kernel(x_ref, sh_ref, sc_ref, g_ref, w_ref, *o_refs, tn):
    yb = _norm_mod(x_ref[...], g_ref[...], sh_ref[0], sc_ref[0]).astype(BF16)
    col = 0
    for o_ref in o_refs:
        n = o_ref.shape[1]
        for j in range(0, n, tn):
            w = min(tn, n - j)
            o_ref[:, j:j + w] = _dot(yb, w_ref[:, col + j:col + j + w])
        col += n


def norm_mod_proj(xall, g, shift, scale, w_pad, ctx_len, tm=256, tn=512):
    nb, ltot, d = xall.shape
    per_b = ltot // tm
    mod_map = _mod_map(per_b, ctx_len // tm, nb)
    outs = pl.pallas_call(
        functools.partial(_inproj_kernel, tn=tn),
        grid=(nb * per_b,),
        in_specs=[pl.BlockSpec((tm, d), lambda i: (i, 0)),
                  pl.BlockSpec((1, 1, d), mod_map),
                  pl.BlockSpec((1, 1, d), mod_map),
                  _full(1, d),
                  _full(d, w_pad.shape[1])],
        out_specs=[pl.BlockSpec((tm, w), lambda i: (i, 0)) for w in MIX_W],
        out_shape=[jax.ShapeDtypeStruct((nb * ltot, w), F32) for w in MIX_W],
        compiler_params=_params("arbitrary"),
        name="norm_mod_proj",
    )(xall.reshape(nb * ltot, d), shift.reshape(nb + 1, 1, d), scale.reshape(nb + 1, 1, d), g.reshape(1, d), w_pad)
    return [o.reshape(nb, ltot, w) for o, w in zip(outs, MIX_W)]


def pad_in_proj(w_in):
    cols, c0 = [], 0
    for n, w in zip(MIX_COLS, MIX_W):
        cols.append(jnp.pad(w_in[:, c0:c0 + n], ((0, 0), (0, w - n))))
        c0 += n
    return jnp.concatenate(cols, axis=1).astype(BF16)


def _outproj_kernel(ya_ref, yb_ref, yc_ref, yd_ref, x_ref, gate_ref, g_ref, w_ref, o_ref):
    p = None
    for m, y_ref in enumerate((ya_ref, yb_ref, yc_ref, yd_ref)):
        t = _dot(y_ref[...].astype(BF16), w_ref[m * D_GROUP:(m + 1) * D_GROUP, :])
        p = t if p is None else p + t
    ms = jnp.mean(p * p, axis=-1, keepdims=True)
    o_ref[...] = x_ref[...] + gate_ref[0] * (p * lax.rsqrt(ms + EPS) * g_ref[...])


def proj_norm_residual(ys, xall, gate, g, w_bf16, ctx_len, tm=256):
    nb, ltot, d = xall.shape
    per_b = ltot // tm
    mod_map = _mod_map(per_b, ctx_len // tm, nb)
    row = lambda w: pl.BlockSpec((tm, w), lambda i: (i, 0))
    out = pl.pallas_call(
        _outproj_kernel,
        grid=(nb * per_b,),
        in_specs=[row(D_GROUP)] * N_MIXERS + [row(d), pl.BlockSpec((1, 1, d), mod_map), _full(1, d), _full(D_MIX, d)],
        out_specs=row(d),
        out_shape=jax.ShapeDtypeStruct((nb * ltot, d), F32),
        compiler_params=_params("arbitrary"),
        name="proj_norm_residual",
    )(*[y.reshape(nb * ltot, D_GROUP) for y in ys], xall.reshape(nb * ltot, d), gate.reshape(nb + 1, 1, d),
      g.reshape(1, d), w_bf16)
    return out.reshape(nb, ltot, d)


def _router_kernel(x_ref, sh_ref, sc_ref, g_ref, r_ref, h_ref, aff_ref):
    h = _norm_mod(x_ref[...], g_ref[...], sh_ref[0], sc_ref[0])
    h_ref[...] = h.astype(h_ref.dtype)
    logits = _dot(h, r_ref[...], prec=HIGHEST)
    m = jnp.max(logits, axis=-1, keepdims=True)
    e = jnp.exp(logits - m)
    aff_ref[...] = e / jnp.sum(e, axis=-1, keepdims=True)


def norm_mod_router(x, g, shift, scale, router, tm=256):
    nb, l_, d = x.shape
    e = router.shape[1]
    tm = min(tm, l_)
    per_b = l_ // tm
    nmod = shift.shape[0]
    mod_map = (lambda i: (i // per_b, 0, 0)) if nmod > 1 else (lambda i: (0, 0, 0))
    h, aff = pl.pallas_call(
        _router_kernel,
        grid=(nb * per_b,),
        in_specs=[pl.BlockSpec((tm, d), lambda i: (i, 0)),
                  pl.BlockSpec((1, 1, d), mod_map),
                  pl.BlockSpec((1, 1, d), mod_map),
                  _full(1, d),
                  _full(d, e)],
        out_specs=[pl.BlockSpec((tm, d), lambda i: (i, 0)),
                   pl.BlockSpec((tm, e), lambda i: (i, 0))],
        out_shape=[jax.ShapeDtypeStruct((nb * l_, d), BF16),
                   jax.ShapeDtypeStruct((nb * l_, e), F32)],
        compiler_params=_params("arbitrary"),
        name="norm_mod_router",
    )(x.reshape(nb * l_, d), shift.reshape(nmod, 1, d), scale.reshape(nmod, 1, d), g.reshape(1, d), router)
    return h.reshape(nb, l_, d), aff.reshape(nb, l_, e)


def _ffn_kernel(x_ref, wg_ref, wu_ref, wd_ref, o_ref):
    x = x_ref[0]
    a = _dot(x, wg_ref[0])
    u = _dot(x, wu_ref[0])
    o_ref[0] = _dot((_silu(a) * u).astype(BF16), wd_ref[0])


def expert_ffn(xs, wg, wu, wd, tm=512):
    e, r, d = xs.shape
    f = wg.shape[2]
    tm = min(tm, r)
    return pl.pallas_call(
        _ffn_kernel,
        grid=(e, r // tm),
        in_specs=[pl.BlockSpec((1, tm, d), lambda i, j: (i, j, 0)),
                  pl.BlockSpec((1, d, f), lambda i, j: (i, 0, 0)),
                  pl.BlockSpec((1, d, f), lambda i, j: (i, 0, 0)),
                  pl.BlockSpec((1, f, d), lambda i, j: (i, 0, 0))],
        out_specs=pl.BlockSpec((1, tm, d), lambda i, j: (i, j, 0)),
        out_shape=jax.ShapeDtypeStruct((e, r, d), F32),
        compiler_params=_params("arbitrary", "arbitrary"),
        name="expert_ffn",
    )(xs, wg, wu, wd)


def _tri(reverse):
    r, c = _iota((CHUNK, CHUNK), 0), _iota((CHUNK, CHUNK), 1)
    return jnp.where((r <= c) if reverse else (r >= c), 1.0, 0.0).astype(F32)


def _block_mask(shape, rblk, cblk):
    return (_iota(shape, 0) // rblk) == (_iota(shape, 1) // cblk)


def _tile_order(reverse, nt):
    if reverse:
        return lambda j: (0, jnp.where(j == 0, 0, nt - j), 0)
    return lambda j: (0, j, 0)


def _tile_index(reverse, nt):
    j = pl.program_id(0)
    return jnp.where(j == 0, 0, nt - j) if reverse else j


def _halo_specs(nb, tile, w, reverse, nt):
    per = tile // HALO
    main = _tile_order(reverse, nt)
    last_blk = nt * per - 1

    def prev(j):
        return (0, jnp.maximum(main(j)[1] * per - 1, 0), 0)

    def nxt(j):
        return (0, jnp.minimum((main(j)[1] + 1) * per, last_blk), 0)

    return [pl.BlockSpec((nb, tile, w), main), pl.BlockSpec((nb, HALO, w), prev), pl.BlockSpec((nb, HALO, w), nxt)]


def _short_conv_tile(x, prev_row, next_row, w_ref, bias):
    t = x.shape[0]
    r = _iota(x.shape, 0)
    xm1 = jnp.where(r == 0, prev_row, pltpu.roll(x, 1, 0))
    xp1 = jnp.where(r == t - 1, next_row, pltpu.roll(x, t - 1, 0))
    y = xm1 * w_ref[0:1, :] + x * w_ref[1:2, :] + xp1 * w_ref[2:3, :]
    return y if bias is None else y + bias


def _conv_halo_ok(t, nt):
    return jnp.where(t >= 2, 1.0, 0.0), jnp.where((t >= 1) & (t <= nt - 2), 1.0, 0.0)


def _scan_calls(kernel_fn, name, u_specs_fn, u_args, param_specs, params, scratch, nb, ltot, tile):
    nt = ltot // tile

    def call(reverse, extra):
        idx = _tile_order(reverse, nt)
        in_specs = u_specs_fn(reverse, nt) + list(param_specs)
        if reverse:
            in_specs.append(pl.BlockSpec((nb, tile, D_GROUP), idx))
        return pl.pallas_call(
            functools.partial(kernel_fn, reverse=reverse, nb=nb, nchunks=tile // CHUNK, nt=nt),
            grid=(nt,),
            in_specs=in_specs,
            out_specs=pl.BlockSpec((nb, tile, D_GROUP), idx),
            out_shape=jax.ShapeDtypeStruct((nb, ltot, D_GROUP), F32),
            scratch_shapes=scratch,
            compiler_params=_params("arbitrary"),
            name=name + ("_bwd" if reverse else "_fwd"),
        )(*u_args, *params, *extra)

    return call(True, (call(False, ()),))


def _gla_kernel(u_ref, w2_ref, b_ref, nw_ref, *rest, reverse, nb, nchunks, nt):
    if reverse:
        of_ref, o_ref, s_ref = rest
    else:
        o_ref, s_ref = rest
    nq = GLA_HEADS * GLA_DK

    @pl.when(pl.program_id(0) == 0)
    def _():
        s_ref[...] = jnp.zeros_like(s_ref)

    d = 1 if reverse else 0
    w2 = w2_ref[d]
    bias = b_ref[d]
    tri = _tri(reverse)
    r64, c256 = _iota((CHUNK, D_GROUP), 0), _iota((CHUNK, D_GROUP), 1) % CHUNK
    causal = (r64 <= c256) if reverse else (r64 >= c256)
    mask_k = _block_mask((D_GROUP, nq), CHUNK, GLA_DK)
    mask_v = _block_mask((D_GROUP, D_GROUP), CHUNK, GLA_DV)
    avg = jnp.where(mask_v, 1.0 / GLA_DV, 0.0).astype(F32)
    last = 0 if reverse else CHUNK - 1
    order = range(nchunks - 1, -1, -1) if reverse else range(nchunks)
    for c in order:
        rows = pl.ds(c * CHUNK, CHUNK)
        for b in range(nb):
            q = u_ref[b, rows, 0:nq] * (GLA_DK ** -0.5)
            k = u_ref[b, rows, nq:2 * nq]
            v = u_ref[b, rows, 2 * nq:2 * nq + D_GROUP]
            lr0 = 2 * nq + 2 * D_GROUP + d * GLA_RANK
            lr = u_ref[b, rows, lr0:lr0 + GLA_RANK]
            z = _dot(lr, w2, prec=HIGHEST) + bias
            g = jax.nn.log_sigmoid(z) / GLA_TAU
            cum = _dot(tri, g, prec=HIGHEST)
            cum_last = cum[last:last + 1, :]
            qg = (q * jnp.exp(cum)).astype(BF16)
            kg = k * jnp.exp(-cum)
            kend = (k * jnp.exp(cum_last - cum)).astype(BF16)
            dec = jnp.exp(cum_last)
            k_st = jnp.where(mask_k, jnp.concatenate([kg] * GLA_HEADS, axis=0), 0.0).astype(BF16)
            attn = jnp.where(causal, _dot(qg, k_st, NT), 0.0).astype(BF16)
            vb = v.astype(BF16)
            v_bd = jnp.where(mask_v, jnp.concatenate([vb] * GLA_HEADS, axis=0), jnp.zeros((), BF16))
            s_t = s_ref[b]
            o = _dot(attn, v_bd) + _dot(qg, s_t.astype(BF16), NT)
            s_ref[b] = s_t * dec + jnp.where(mask_k, _dot(vb, kend, TN), 0.0)
            if reverse:
                o = o + of_ref[b, rows, :]
                ms = _dot(o * o, avg, prec=HIGHEST)
                og = u_ref[b, rows, 2 * nq + D_GROUP:2 * nq + 2 * D_GROUP]
                o = o * lax.rsqrt(ms + EPS) * nw_ref[...] * _silu(og)
            o_ref[b, rows, :] = o


def gla_mixer(u, w2, b, norm_w, tile):
    nb, ltot, w = u.shape
    nq = GLA_HEADS * GLA_DK
    u_specs = lambda reverse, nt: [pl.BlockSpec((nb, tile, w), _tile_order(reverse, nt))]
    return _scan_calls(_gla_kernel, "gla", u_specs, (u,),
                       [_full(2, GLA_RANK, nq), _full(2, 1, nq), _full(1, D_GROUP)],
                       (w2, b.reshape(2, 1, nq), norm_w.reshape(1, D_GROUP)),
                       [pltpu.VMEM((nb, D_GROUP, nq), F32)], nb, ltot, tile)


def _ssd_kernel(u_ref, up_ref, un_ref, cw_ref, cb_ref, alog_ref, dtb_ref, dsk_ref, nw_ref, *rest,
                reverse, nb, nchunks, nt):
    if reverse:
        of_ref, o_ref, s_ref, xbc_ref = rest
    else:
        o_ref, s_ref, xbc_ref = rest
    t = _tile_index(reverse, nt)

    @pl.when(pl.program_id(0) == 0)
    def _():
        s_ref[...] = jnp.zeros_like(s_ref)

    prev_ok, next_ok = _conv_halo_ok(t, nt)
    c0, c1 = D_GROUP, D_GROUP + MB_CONV_CH
    for b in range(nb):
        x = u_ref[b, :, c0:c1]
        pr = up_ref[b, HALO - 1:HALO, c0:c1] * prev_ok
        nx = un_ref[b, 0:1, c0:c1] * next_ok
        xbc_ref[b] = _silu(_short_conv_tile(x, pr, nx, cw_ref, cb_ref[...]))

    d = 1 if reverse else 0
    a = -jnp.exp(alog_ref[d:d + 1, :])
    dtb = dtb_ref[d:d + 1, :]
    tri = _tri(reverse)
    r64, c256 = _iota((CHUNK, D_GROUP), 0), _iota((CHUNK, D_GROUP), 1) % CHUNK
    causal = (r64 <= c256) if reverse else (r64 >= c256)
    tri_t = jnp.where((r64 >= c256) if reverse else (r64 <= c256), 1.0, 0.0).astype(F32)
    e256 = jnp.where(_iota((MB_HEADS, D_GROUP), 0) == _iota((MB_HEADS, D_GROUP), 1) // CHUNK, 1.0, 0.0).astype(F32)
    hn = MB_HEADS * MB_STATE
    e512 = jnp.where(_iota((MB_HEADS, hn), 0) == _iota((MB_HEADS, hn), 1) // MB_STATE, 1.0, 0.0).astype(F32)
    gw = MB_GROUPS * MB_STATE
    rep = MB_HEADS // MB_GROUPS
    mask_b = (_iota((D_GROUP, gw), 0) // (CHUNK * rep)) == (_iota((D_GROUP, gw), 1) // MB_STATE)
    mask_v = _block_mask((D_GROUP, D_GROUP), CHUNK, MB_HEADDIM)
    mask_s = _block_mask((hn, D_GROUP), MB_STATE, MB_HEADDIM)
    avg = jnp.where(mask_v, 1.0 / MB_HEADDIM, 0.0).astype(F32)
    last = 0 if reverse else CHUNK - 1
    order = range(nchunks - 1, -1, -1) if reverse else range(nchunks)
    for c in order:
        rows = pl.ds(c * CHUNK, CHUNK)
        for b in range(nb):
            xs = xbc_ref[b, rows, 0:D_GROUP]
            bm = xbc_ref[b, rows, D_GROUP:D_GROUP + gw]
            cm = xbc_ref[b, rows, D_GROUP + gw:D_GROUP + 2 * gw]
            dt0 = D_GROUP + MB_CONV_CH + d * MB_HEADS
            dt = jax.nn.softplus(u_ref[b, rows, dt0:dt0 + MB_HEADS] + dtb)
            da = dt * a
            cum4 = _dot(tri, da, prec=HIGHEST)
            da_b = _dot(da, e256, prec=HIGHEST)
            cum_b = _dot(cum4, e256, prec=HIGHEST)
            cum_row = jnp.sum(da_b * tri_t, axis=0, keepdims=True)
            seg = jnp.exp(jnp.where(causal, cum_b - cum_row, -jnp.inf))
            cum512 = _dot(cum4, e512, prec=HIGHEST)
            cum_last512 = cum512[last:last + 1, :]
            xdt = xs * _dot(dt, e256, prec=HIGHEST)
            xdt_b = xdt.astype(BF16)
            b_st = jnp.where(mask_b, jnp.concatenate([bm] * MB_HEADS, axis=0), 0.0).astype(BF16)
            lmat = (_dot(cm.astype(BF16), b_st, NT) * seg).astype(BF16)
            xdt_bd = jnp.where(mask_v, jnp.concatenate([xdt_b] * MB_HEADS, axis=0), jnp.zeros((), BF16))
            c_cat = jnp.concatenate([cm[:, (h // rep) * MB_STATE:(h // rep + 1) * MB_STATE] for h in range(MB_HEADS)], 1)
            b_cat = jnp.concatenate([bm[:, (h // rep) * MB_STATE:(h // rep + 1) * MB_STATE] for h in range(MB_HEADS)], 1)
            c_cat = (c_cat * jnp.exp(cum512)).astype(BF16)
            b_cat = (b_cat * jnp.exp(cum_last512 - cum512)).astype(BF16)
            s = s_ref[b]
            y = _dot(lmat, xdt_bd) + _dot(c_cat, s.astype(BF16))
            dec = jnp.exp(cum_b[last:last + 1, :])
            s_ref[b] = s * dec + jnp.where(mask_s, _dot(b_cat, xdt_b, TN), 0.0)
            if reverse:
                y = y + of_ref[b, rows, :]
                yy = (y + xs * dsk_ref[...]) * _silu(u_ref[b, rows, 0:D_GROUP])
                ms = _dot(yy * yy, avg, prec=HIGHEST)
                y = yy * lax.rsqrt(ms + EPS) * nw_ref[...]
            o_ref[b, rows, :] = y


def mamba_mixer(u, conv_w, conv_b, a_log, dt_bias, d_skip, norm_w, tile):
    nb, ltot, w = u.shape
    d_cat = jnp.repeat(d_skip.astype(F32), MB_HEADDIM).reshape(1, D_GROUP)
    u_specs = lambda reverse, nt: _halo_specs(nb, tile, w, reverse, nt)
    return _scan_calls(_ssd_kernel, "ssd", u_specs, (u, u, u),
                       [_full(SHORT_CONV, MB_CONV_CH), _full(1, MB_CONV_CH), _full(2, MB_HEADS), _full(2, MB_HEADS),
                        _full(1, D_GROUP), _full(1, D_GROUP)],
                       (conv_w, conv_b.reshape(1, -1), a_log, dt_bias, d_cat, norm_w.reshape(1, D_GROUP)),
                       [pltpu.VMEM((nb, MB_HEADS * MB_STATE, D_GROUP), F32), pltpu.VMEM((nb, tile, MB_CONV_CH), F32)],
                       nb, ltot, tile)


def _to_bd(x_cat, mask):
    return jnp.where(mask, jnp.concatenate([x_cat] * DN_HEADS, axis=0), jnp.zeros((), x_cat.dtype))


def _dn_kernel(u_ref, up_ref, un_ref, cw_ref, alog_ref, dtb_ref, nw_ref, *rest, reverse, nb, nchunks, nt):
    if reverse:
        of_ref, o_ref, s_ref, qkv_ref = rest
    else:
        o_ref, s_ref, qkv_ref = rest
    t = _tile_index(reverse, nt)

    @pl.when(pl.program_id(0) == 0)
    def _():
        s_ref[...] = jnp.zeros_like(s_ref)

    prev_ok, next_ok = _conv_halo_ok(t, nt)
    for b in range(nb):
        x = u_ref[b, :, 0:DN_QKV]
        pr = up_ref[b, HALO - 1:HALO, 0:DN_QKV] * prev_ok
        nx = un_ref[b, 0:1, 0:DN_QKV] * next_ok
        qkv_ref[b] = _silu(_short_conv_tile(x, pr, nx, cw_ref, None))

    d = 1 if reverse else 0
    neg_a = -jnp.exp(alog_ref[d:d + 1, :])
    dtb = dtb_ref[d:d + 1, :]
    tri = _tri(reverse)
    r64, c256 = _iota((CHUNK, D_GROUP), 0), _iota((CHUNK, D_GROUP), 1) % CHUNK
    incl = (r64 <= c256) if reverse else (r64 >= c256)
    strict = (r64 < c256) if reverse else (r64 > c256)
    tri_t = jnp.where((r64 >= c256) if reverse else (r64 <= c256), 1.0, 0.0).astype(F32)
    eye_cat = jnp.where(r64 == c256, 1.0, 0.0).astype(F32)
    e256 = jnp.where(_iota((DN_HEADS, D_GROUP), 0) == _iota((DN_HEADS, D_GROUP), 1) // CHUNK, 1.0, 0.0).astype(F32)
    mask = _block_mask((D_GROUP, D_GROUP), CHUNK, DN_DK)
    ones_blk = jnp.where(mask, 1.0, 0.0).astype(F32)
    avg = ones_blk * (1.0 / DN_DV)
    last = 0 if reverse else CHUNK - 1
    order = range(nchunks - 1, -1, -1) if reverse else range(nchunks)
    for c in order:
        rows = pl.ds(c * CHUNK, CHUNK)
        for b in range(nb):
            q = qkv_ref[b, rows, 0:D_GROUP]
            k = qkv_ref[b, rows, D_GROUP:2 * D_GROUP]
            v = qkv_ref[b, rows, 2 * D_GROUP:3 * D_GROUP]
            qn = q * lax.rsqrt(_dot(q * q, ones_blk, prec=HIGHEST) + EPS) * (DN_DK ** -0.5)
            kn = k * lax.rsqrt(_dot(k * k, ones_blk, prec=HIGHEST) + EPS)
            b0 = 4 * D_GROUP + d * DN_HEADS
            a0 = 4 * D_GROUP + N_DIR * DN_HEADS + d * DN_HEADS
            beta = jax.nn.sigmoid(u_ref[b, rows, b0:b0 + DN_HEADS])
            g = neg_a * jax.nn.softplus(u_ref[b, rows, a0:a0 + DN_HEADS] + dtb)
            cum4 = _dot(tri, g, prec=HIGHEST)
            g_b = _dot(g, e256, prec=HIGHEST)
            cum_b = _dot(cum4, e256, prec=HIGHEST)
            beta_b = _dot(beta, e256, prec=HIGHEST)
            cum_row = jnp.sum(g_b * tri_t, axis=0, keepdims=True)
            decay = jnp.exp(jnp.where(incl, cum_b - cum_row, -jnp.inf))
            cum_last = cum_b[last:last + 1, :]
            e_cum = jnp.exp(cum_b)
            k_beta = kn * beta_b
            k_st = _to_bd(kn.astype(BF16), mask)
            m_cat = jnp.where(strict, _dot(k_beta.astype(BF16), k_st, NT) * decay, 0.0)
            attn = (_dot(qn.astype(BF16), k_st, NT) * decay).astype(BF16)
            t_cat = eye_cat - m_cat
            pw = _dot(m_cat.astype(BF16), _to_bd(m_cat.astype(BF16), mask))
            for _ in range(4):
                both = _dot(jnp.concatenate([t_cat, pw], axis=0).astype(BF16), _to_bd(pw.astype(BF16), mask))
                t_cat = t_cat + both[0:CHUNK]
                pw = both[CHUNK:2 * CHUNK]
            t_cat = t_cat + _dot(t_cat.astype(BF16), _to_bd(pw.astype(BF16), mask))
            rhs = jnp.concatenate([_to_bd((v * beta_b).astype(BF16), mask),
                                   _to_bd((k_beta * e_cum).astype(BF16), mask)], axis=1)
            sol = _dot(t_cat.astype(BF16), rhs)
            u_val, w_key = sol[:, 0:D_GROUP], sol[:, D_GROUP:2 * D_GROUP]
            s = s_ref[b]
            s_b = s.astype(BF16)
            v_new = u_val - _dot(w_key.astype(BF16), s_b)
            v_new_b = v_new.astype(BF16)
            o = _dot((qn * e_cum).astype(BF16), s_b) + _dot(attn, _to_bd(v_new_b, mask))
            k_end = (kn * jnp.exp(cum_last - cum_b)).astype(BF16)
            s_ref[b] = s * jnp.exp(cum_last) + jnp.where(mask, _dot(k_end, v_new_b, TN), 0.0)
            if reverse:
                o = o + of_ref[b, rows, :]
                ms = _dot(o * o, avg, prec=HIGHEST)
                og = u_ref[b, rows, 3 * D_GROUP:4 * D_GROUP]
                o = o * lax.rsqrt(ms + EPS) * nw_ref[...] * _silu(og)
            o_ref[b, rows, :] = o


def deltanet_mixer(u, conv_w, a_log, dt_bias, norm_w, tile):
    nb, ltot, w = u.shape
    u_specs = lambda reverse, nt: _halo_specs(nb, tile, w, reverse, nt)
    return _scan_calls(_dn_kernel, "dn", u_specs, (u, u, u),
                       [_full(SHORT_CONV, DN_QKV), _full(2, DN_HEADS), _full(2, DN_HEADS), _full(1, D_GROUP)],
                       (conv_w, a_log, dt_bias, norm_w.reshape(1, D_GROUP)),
                       [pltpu.VMEM((nb, D_GROUP, D_GROUP), F32), pltpu.VMEM((nb, tile, DN_QKV), F32)],
                       nb, ltot, tile)


def short_conv(u, w, b=None):
    k_, ch = w.shape
    y = lax.conv_general_dilated(u, w[:, None, :].astype(u.dtype), window_strides=(1,),
                                 padding=[(k_ // 2, k_ // 2)], dimension_numbers=('NWC', 'WIO', 'NWC'),
                                 feature_group_count=ch)
    if b is not None:
        y = y + b.astype(u.dtype)
    return y


def hyena_filters(l_, w1, b1, w2, b2, w3, b3, w4, freq):
    pos = jnp.arange(l_, dtype=F32)
    t = pos / max(l_ - 1, 1)
    bands = jnp.linspace(1e-4, HY_BANDS - 1, HY_BANDS, dtype=F32)
    ang = (2.0 * math.pi / l_) * pos[:, None] * bands[None, :]
    feats = jnp.concatenate([t[:, None], jnp.cos(ang), -jnp.sin(ang)], axis=-1)
    f = freq.astype(F32)
    h = jnp.sin(f[0] * (feats @ w1.astype(F32) + b1.astype(F32)))
    h = jnp.sin(f[1] * (h @ w2.astype(F32) + b2.astype(F32)))
    h = jnp.sin(f[2] * (h @ w3.astype(F32) + b3.astype(F32)))
    h = h @ w4.astype(F32)
    max_decay = math.log(HY_DECAY_TARGET) / HY_FAST_DECAY
    min_decay = math.log(HY_DECAY_TARGET) / HY_SLOW_DECAY
    deltas = jnp.abs(jnp.linspace(min_decay, max_decay, D_GROUP, dtype=F32))
    window = jnp.exp(-t[:, None] * deltas[None, :])
    return h.reshape(l_, HY_ORDER, N_DIR, D_GROUP) * window[:, None, None, :]


def long_conv(u, h_fwd, h_bwd, d_skip):
    l_, ch = h_fwd.shape
    kern = jnp.concatenate([h_fwd, jnp.zeros((1, ch), F32), jnp.flip(h_bwd[1:], 0)], axis=0)
    spec = jnp.fft.rfft(u, n=2 * l_, axis=1) * jnp.fft.rfft(kern, axis=0)[None]
    y = jnp.fft.irfft(spec, n=2 * l_, axis=1)[:, :l_]
    return y + u * d_skip


def hyena_mixer(u_ctx, u_lat, conv_w, conv_b, w1, b1, w2, b2, w3, b3, w4, freq, d_skip, need_ctx):
    d = d_skip.astype(F32)

    def run(u):
        l_ = u.shape[1]
        uc = short_conv(u, conv_w, conv_b).astype(F32)
        v, x1, x2 = jnp.split(uc, [D_GROUP, 2 * D_GROUP], axis=-1)
        h = hyena_filters(l_, w1, b1, w2, b2, w3, b3, w4, freq)
        z = x1 * long_conv(v, h[:, 0, 0], h[:, 0, 1], d[0])
        return (x2 * long_conv(z, h[:, 1, 0], h[:, 1, 1], d[1])).astype(u.dtype)

    return (run(u_ctx) if need_ctx else jnp.zeros(u_ctx.shape[:2] + (D_GROUP,), F32)), run(u_lat)


def to_col_major(u, rows):
    b_, l_, ch = u.shape
    return u.reshape(b_, rows, GRID_W, ch).transpose(0, 2, 1, 3).reshape(b_, l_, ch)


def to_row_major(u, rows):
    b_, l_, ch = u.shape
    return u.reshape(b_, GRID_W, rows, ch).transpose(0, 2, 1, 3).reshape(b_, l_, ch)


def ec_moe(h_bf16, aff, wg, wu, wd):
    b_, t_, d_ = h_bf16.shape
    cap = EC_CAPACITY * t_ // N_EXPERTS
    weight, idx = lax.top_k(jnp.swapaxes(aff, 1, 2), cap)
    xs = jax.vmap(lambda hb, ib: hb[ib])(h_bf16, idx)
    xs = xs.transpose(1, 0, 2, 3).reshape(N_EXPERTS, b_ * cap, d_)
    y = expert_ffn(xs, wg, wu, wd)
    y = y.reshape(N_EXPERTS, b_, cap, d_).transpose(1, 0, 2, 3) * weight[..., None]
    scatter = lambda ib, yb: jnp.zeros((t_, d_), F32).at[ib.reshape(-1)].add(yb.reshape(-1, d_))
    return jax.vmap(scatter)(idx, y)


def _rmsnorm_residual(x, m, gate, g):
    ms = jnp.mean(m * m, axis=-1, keepdims=True)
    return x + gate[:, None, :] * (m * lax.rsqrt(ms + EPS) * g)


def kernel(x, c, ctx, c_ctx, w_mod, b_mod, norm_g, w_in, w_out, gla_w2, gla_b, gla_norm,
           hy_conv_w, hy_conv_b, hy_w1, hy_b1, hy_w2, hy_b2, hy_w3, hy_b3, hy_w4, hy_freq, hy_d,
           mb_conv_w, mb_conv_b, mb_a_log, mb_dt_bias, mb_d, mb_norm,
           dn_conv_w, dn_a_log, dn_dt_bias, dn_norm,
           moe_router, moe_w_gate, moe_w_up, moe_w_down):
    b_, seq, d_ = x.shape
    lc = ctx.shape[1]
    rows = seq // GRID_W
    xall = jnp.concatenate([ctx, x], axis=1)
    cond = jnp.concatenate([c, c_ctx[None, :], jnp.zeros((HALO - b_ - 1, d_), F32)], axis=0)
    for i in range(DEPTH):
        need_ctx = i < DEPTH - 1
        mod = modulation_all(cond, w_mod[i], b_mod[i])[:b_ + 1]
        sh1, sc1, g1, sh2, sc2, g2 = jnp.split(mod, N_MOD, axis=-1)
        w_out_b = w_out[i].astype(BF16)

        ua, ub, uc, ud = norm_mod_proj(xall, norm_g[i, 0], sh1, sc1, pad_in_proj(w_in[i]), lc)
        ya = gla_mixer(ua, gla_w2[i], gla_b[i], gla_norm[i], lc)
        yb_c, yb_l = hyena_mixer(ub[:, :lc], ub[:, lc:], hy_conv_w[i], hy_conv_b[i], hy_w1[i], hy_b1[i], hy_w2[i],
                                 hy_b2[i], hy_w3[i], hy_b3[i], hy_w4[i], hy_freq[i], hy_d[i], need_ctx)
        yb = jnp.concatenate([yb_c, yb_l], axis=1)
        uc = jnp.concatenate([uc[:, :lc], to_col_major(uc[:, lc:], rows)], axis=1)
        yc = mamba_mixer(uc, mb_conv_w[i], mb_conv_b[i], mb_a_log[i], mb_dt_bias[i], mb_d[i], mb_norm[i], lc)
        yc = jnp.concatenate([yc[:, :lc], to_row_major(yc[:, lc:], rows)], axis=1)
        yd = deltanet_mixer(ud, dn_conv_w[i], dn_a_log[i], dn_dt_bias[i], dn_norm[i], lc)
        xall = proj_norm_residual((ya, yb, yc, yd), xall, g1, norm_g[i, 1], w_out_b, lc)

        wg = moe_w_gate[i].astype(BF16)
        wu = moe_w_up[i].astype(BF16)
        wd = moe_w_down[i].astype(BF16)
        x_l = xall[:, lc:]
        h_l, aff_l = norm_mod_router(x_l, norm_g[i, 2], sh2[:b_], sc2[:b_], moe_router[i])
        x_l = _rmsnorm_residual(x_l, ec_moe(h_l, aff_l, wg, wu, wd), g2[:b_], norm_g[i, 3])
        if need_ctx:
            x_c = xall[:, :lc]
            h_c, aff_c = norm_mod_router(x_c, norm_g[i, 2], sh2[b_:], sc2[b_:], moe_router[i])
            g2c = jnp.broadcast_to(g2[b_:], (b_, d_))
            x_c = _rmsnorm_residual(x_c, ec_moe(h_c, aff_c, wg, wu, wd), g2c, norm_g[i, 3])
            xall = jnp.concatenate([x_c, x_l], axis=1)
        else:
            return x_l
    return xall[:, lc:]
```

```python
import math, functools
import jax
import jax.numpy as jnp
from jax import lax
from jax.experimental import pallas as pl
from jax.experimental.pallas import tpu as pltpu

F32 = jnp.float32
BF16 = jnp.bfloat16

D_MODEL = 1024
DEPTH = 2
GRID_W = 64
N_MIXERS = 4
D_MIX = D_MODEL
D_GROUP = D_MIX // N_MIXERS
CHUNK = 64
SHORT_CONV = 3
N_DIR = 2
EPS = 1e-6
N_MOD = 6

GLA_HEADS = 4
GLA_DK = D_GROUP // (2 * GLA_HEADS)
GLA_DV = D_GROUP // GLA_HEADS
GLA_RANK = 16
GLA_TAU = 16.0

HY_ORDER = 2
HY_EMB = 33
HY_BANDS = (HY_EMB - 1) // 2
HY_FILTER_HIDDEN = 64
HY_DECAY_TARGET = 1e-2
HY_FAST_DECAY = 0.3
HY_SLOW_DECAY = 1.5

MB_HEADS = 4
MB_HEADDIM = D_GROUP // MB_HEADS
MB_STATE = 128
MB_GROUPS = 2
MB_CONV_CH = D_GROUP + 2 * MB_GROUPS * MB_STATE

DN_HEADS = 4
DN_DK = D_GROUP // DN_HEADS
DN_DV = D_GROUP // DN_HEADS
DN_QKV = 2 * DN_HEADS * DN_DK + DN_HEADS * DN_DV

N_EXPERTS = 16
EC_CAPACITY = 2
D_EXPERT = 1024

GLA_COLS = 2 * GLA_HEADS * GLA_DK + 2 * D_GROUP + N_DIR * GLA_RANK
HY_COLS = (HY_ORDER + 1) * D_GROUP
MB_COLS = D_GROUP + MB_CONV_CH + N_DIR * MB_HEADS
DN_COLS = 4 * D_GROUP + 2 * N_DIR * DN_HEADS
D_IN_PROJ = GLA_COLS + HY_COLS + MB_COLS + DN_COLS

LANES = 128
HALO = 8


def _lane_pad(n):
    return -(-n // LANES) * LANES


MOE_TOK = 256
MOE_SLAB = MOE_TOK + 16
HX_W = D_MODEL + LANES

MIX_COLS = (GLA_COLS, HY_COLS, MB_COLS, DN_COLS)
MIX_W = tuple(_lane_pad(n) for n in MIX_COLS)
GLA_W, HY_W, MB_W, DN_W = MIX_W

VMEM_LIMIT_BYTES = 48 * 1024 * 1024
HIGHEST = lax.Precision.HIGHEST

NN = (((1,), (0,)), ((), ()))
NT = (((1,), (1,)), ((), ()))
TN = (((0,), (0,)), ((), ()))


def _params(*sem):
    return pltpu.CompilerParams(dimension_semantics=sem, vmem_limit_bytes=VMEM_LIMIT_BYTES)


def _dot(a, b, dims=NN, prec=None):
    return lax.dot_general(a, b, dims, precision=prec, preferred_element_type=F32)


def _iota(shape, axis):
    return lax.broadcasted_iota(jnp.int32, shape, axis)


def _silu(x):
    return x * jax.nn.sigmoid(x)


def _full(*shape):
    return pl.BlockSpec(shape, lambda *_: (0,) * len(shape))


def _modulation_kernel(c_ref, w_ref, b_ref, o_ref):
    o_ref[...] = _dot(_silu(c_ref[...]), w_ref[...], prec=HIGHEST) + b_ref[...]


def modulation_all(cond, w_mod, b_mod):
    r, d = cond.shape
    n = w_mod.shape[1]
    tn = 1024
    return pl.pallas_call(
        _modulation_kernel,
        grid=(n // tn,),
        in_specs=[pl.BlockSpec((r, d), lambda j: (0, 0)),
                  pl.BlockSpec((d, tn), lambda j: (0, j)),
                  pl.BlockSpec((1, tn), lambda j: (0, j))],
        out_specs=pl.BlockSpec((r, tn), lambda j: (0, j)),
        out_shape=jax.ShapeDtypeStruct((r, n), F32),
        compiler_params=_params("arbitrary"),
        name="modulation",
    )(cond, w_mod, b_mod.reshape(1, n))


def _norm_mod(x, g, sh, sc):
    ms = jnp.mean(x * x, axis=-1, keepdims=True)
    y = x * lax.rsqrt(ms + EPS) * g
    return y * (1.0 + sc) + sh


def _mod_map(per_b, ctx_tiles, nb):
    return lambda i: (jnp.where(i % per_b >= per_b - ctx_tiles, nb, i // per_b), 0, 0)


def _inproj_kernel(x_ref, sh_ref, sc_ref, g_ref, w_ref, *o_refs, tn):
    yb = _norm_mod(x_ref[...], g_ref[...], sh_ref[0], sc_ref[0]).astype(BF16)
    col = 0
    for o_ref in o_refs:
        n = o_ref.shape[1]
        for j in range(0, n, tn):
            w = min(tn, n - j)
            o_ref[:, j:j + w] = _dot(yb, w_ref[:, col + j:col + j + w])
        col += n


def norm_mod_proj(xall, g, shift, scale, w_pad, ctx_len, tm=256, tn=512):
    nb, ltot, d = xall.shape
    per_b = ltot // tm
    mod_map = _mod_map(per_b, ctx_len // tm, nb)
    outs = pl.pallas_call(
        functools.partial(_inproj_kernel, tn=tn),
        grid=(nb * per_b,),
        in_specs=[pl.BlockSpec((tm, d), lambda i: (i, 0)),
                  pl.BlockSpec((1, 1, d), mod_map),
                  pl.BlockSpec((1, 1, d), mod_map),
                  _full(1, d),
                  _full(d, w_pad.shape[1])],
        out_specs=[pl.BlockSpec((tm, w), lambda i: (i, 0)) for w in MIX_W],
        out_shape=[jax.ShapeDtypeStruct((nb * ltot, w), F32) for w in MIX_W],
        compiler_params=_params("arbitrary"),
        name="norm_mod_proj",
    )(xall.reshape(nb * ltot, d), shift.reshape(nb + 1, 1, d), scale.reshape(nb + 1, 1, d), g.reshape(1, d), w_pad)
    return [o.reshape(nb, ltot, w) for o, w in zip(outs, MIX_W)]


def pad_in_proj(w_in):
    cols, c0 = [], 0
    for n, w in zip(MIX_COLS, MIX_W):
        cols.append(jnp.pad(w_in[:, c0:c0 + n], ((0, 0), (0, w - n))))
        c0 += n
    return jnp.concatenate(cols, axis=1).astype(BF16)


def _outproj_kernel(ya_ref, yb_ref, yc_ref, yd_ref, x_ref, gate_ref, g_ref, w_ref, o_ref):
    p = None
    for m, y_ref in enumerate((ya_ref, yb_ref, yc_ref, yd_ref)):
        t = _dot(y_ref[...].astype(BF16), w_ref[m * D_GROUP:(m + 1) * D_GROUP, :])
        p = t if p is None else p + t
    ms = jnp.mean(p * p, axis=-1, keepdims=True)
    o_ref[...] = x_ref[...] + gate_ref[0] * (p * lax.rsqrt(ms + EPS) * g_ref[...])


def proj_norm_residual(ys, xall, gate, g, w_bf16, ctx_len, tm=256):
    nb, ltot, d = xall.shape
    per_b = ltot // tm
    mod_map = _mod_map(per_b, ctx_len // tm, nb)
    row = lambda w: pl.BlockSpec((tm, w), lambda i: (i, 0))
    out = pl.pallas_call(
        _outproj_kernel,
        grid=(nb * per_b,),
        in_specs=[row(D_GROUP)] * N_MIXERS + [row(d), pl.BlockSpec((1, 1, d), mod_map), _full(1, d), _full(D_MIX, d)],
        out_specs=row(d),
        out_shape=jax.ShapeDtypeStruct((nb * ltot, d), F32),
        compiler_params=_params("arbitrary"),
        name="proj_norm_residual",
    )(*[y.reshape(nb * ltot, D_GROUP) for y in ys], xall.reshape(nb * ltot, d), gate.reshape(nb + 1, 1, d),
      g.reshape(1, d), w_bf16)
    return out.reshape(nb, ltot, d)


def _router_kernel(x_ref, sh_ref, sc_ref, g_ref, r_ref, rt_ref, hx_ref, afft_ref):
    h = _norm_mod(x_ref[...], g_ref[...], sh_ref[0], sc_ref[0])
    hx_ref[:, 0:D_MODEL] = h.astype(BF16)
    logits = _dot(h, r_ref[...], prec=HIGHEST)
    e = jnp.exp(logits - jnp.max(logits, axis=-1, keepdims=True))
    aff = e / jnp.sum(e, axis=-1, keepdims=True)
    a1 = aff.astype(BF16)
    r1 = aff - a1.astype(F32)
    a2 = r1.astype(BF16)
    a3 = (r1 - a2.astype(F32)).astype(BF16)
    pad = jnp.zeros((aff.shape[0], LANES - 3 * N_EXPERTS), BF16)
    hx_ref[:, D_MODEL:HX_W] = jnp.concatenate([a1, a2, a3, pad], axis=1)
    lt = _dot(rt_ref[...], h, NT, prec=HIGHEST)
    et = jnp.exp(lt - jnp.max(lt, axis=0, keepdims=True))
    afft_ref[0] = et / jnp.sum(et, axis=0, keepdims=True)


def norm_mod_router(xall, g, shift, scale, router, ctx_len, tm=256):
    nb, ltot, d = xall.shape
    e = router.shape[1]
    per_b = ltot // tm
    mod_map = _mod_map(per_b, ctx_len // tm, nb)
    hx, afft = pl.pallas_call(
        _router_kernel,
        grid=(nb * per_b,),
        in_specs=[pl.BlockSpec((tm, d), lambda i: (i, 0)),
                  pl.BlockSpec((1, 1, d), mod_map),
                  pl.BlockSpec((1, 1, d), mod_map),
                  _full(1, d), _full(d, e), _full(e, d)],
        out_specs=[pl.BlockSpec((tm, HX_W), lambda i: (i, 0)),
                   pl.BlockSpec((1, e, tm), lambda i: (i // per_b, 0, i % per_b))],
        out_shape=[jax.ShapeDtypeStruct((nb * ltot, HX_W), BF16),
                   jax.ShapeDtypeStruct((nb, e, ltot), F32)],
        compiler_params=_params("arbitrary"),
        name="norm_mod_router",
    )(xall.reshape(nb * ltot, d), shift.reshape(nb + 1, 1, d), scale.reshape(nb + 1, 1, d), g.reshape(1, d),
      router, router.T)
    return hx.reshape(nb, ltot, HX_W), afft


def _cap_pad(t_):
    cap = EC_CAPACITY * t_ // N_EXPERTS
    return cap, -(-(cap + MOE_SLAB) // MOE_TOK) * MOE_TOK


def _lane_prefix(x, t_):
    upper = jnp.where(_iota((LANES, LANES), 0) <= _iota((LANES, LANES), 1), 1.0, 0.0).astype(BF16)
    carry = jnp.zeros((x.shape[0], 1), F32)
    out = []
    for j in range(t_ // LANES):
        p = _dot(x[:, j * LANES:(j + 1) * LANES].astype(BF16), upper) + carry
        out.append(p)
        carry = p[:, LANES - 1:LANES]
    return jnp.concatenate(out, axis=1)


def _select_kernel(afft_ref, pos_ref, off_ref, *, cap, t_):
    key = pltpu.bitcast(afft_ref[0], jnp.int32)
    thr = jnp.zeros((key.shape[0], 1), jnp.int32)
    for bit in range(30, -1, -1):
        cand = thr | (1 << bit)
        cnt = jnp.sum(jnp.where(key >= cand, 1, 0), axis=1, keepdims=True)
        thr = jnp.where(cnt >= cap, cand, thr)
    gt = key > thr
    eq = key == thr
    need = (cap - jnp.sum(jnp.where(gt, 1, 0), axis=1, keepdims=True)).astype(F32)
    eq_rank = _lane_prefix(jnp.where(eq, 1.0, 0.0), t_)
    sel = gt | (eq & (eq_rank <= need))
    sel_f = jnp.where(sel, 1.0, 0.0)
    slot = _lane_prefix(sel_f, t_) - sel_f
    pos_ref[0] = jnp.where(sel, slot, -1.0).astype(jnp.int32)
    nt = t_ // MOE_TOK
    before = jnp.where(_iota((t_, LANES), 0) < _iota((t_, LANES), 1) * MOE_TOK, 1.0, 0.0).astype(BF16)
    off_ref[0] = _dot(sel_f.astype(BF16), before)[:, 0:nt].astype(jnp.int32)


def select_tokens(afft, t0, t_):
    nb, e, _ = afft.shape
    cap, _ = _cap_pad(t_)
    nt = t_ // MOE_TOK
    return pl.pallas_call(
        functools.partial(_select_kernel, cap=cap, t_=t_),
        grid=(nb,),
        in_specs=[pl.BlockSpec((1, e, t_), lambda b: (b, 0, t0 // t_))],
        out_specs=[pl.BlockSpec((1, e, t_), lambda b: (b, 0, 0)), pl.BlockSpec((1, e, nt), lambda b: (b, 0, 0))],
        out_shape=[jax.ShapeDtypeStruct((nb, e, t_), jnp.int32), jax.ShapeDtypeStruct((nb, e, nt), jnp.int32)],
        compiler_params=_params("arbitrary"),
        name="moe_select",
    )(afft)


def _slot_onehot(off_ref, pos_ref, b, e, j):
    off = pl.multiple_of((off_ref[b, e, j] // 16) * 16, 16)
    pos = pos_ref[0, 0, pl.ds(j, 1), :]
    r = _iota((MOE_SLAB, MOE_TOK), 0)
    return off, jnp.where(pos - off == r, 1.0, 0.0).astype(BF16)


def _gather_kernel(off_ref, pos_ref, hx_ref, xs_ref, acc_ref, *, nt):
    b, e = pl.program_id(0), pl.program_id(1)
    acc_ref[...] = jnp.zeros_like(acc_ref)

    def tile(j, carry):
        off, onehot = _slot_onehot(off_ref, pos_ref, b, e, j)
        rows = hx_ref[0, pl.ds(pl.multiple_of(j * MOE_TOK, MOE_TOK), MOE_TOK), :]
        acc_ref[pl.ds(off, MOE_SLAB), :] += _dot(onehot, rows)
        return carry

    lax.fori_loop(0, nt, tile, 0)
    xs_ref[0, 0] = acc_ref[...].astype(BF16)


def gather_tokens(hx, pos, off, t0):
    nb, _, w = hx.shape
    e, nt = off.shape[1], off.shape[2]
    t_ = nt * MOE_TOK
    _, cpad = _cap_pad(t_)
    return pl.pallas_call(
        functools.partial(_gather_kernel, nt=nt),
        grid_spec=pltpu.PrefetchScalarGridSpec(
            num_scalar_prefetch=1,
            grid=(nb, e),
            in_specs=[pl.BlockSpec((1, 1, nt, MOE_TOK), lambda b, e_, off_: (b, e_, 0, 0)),
                      pl.BlockSpec((1, t_, w), lambda b, e_, off_: (b, t0 // t_, 0), pipeline_mode=pl.Buffered(1))],
            out_specs=pl.BlockSpec((1, 1, cpad, w), lambda b, e_, off_: (b, e_, 0, 0)),
            scratch_shapes=[pltpu.VMEM((cpad, w), F32)]),
        out_shape=jax.ShapeDtypeStruct((nb, e, cpad, w), BF16),
        compiler_params=_params("arbitrary", "arbitrary"),
        name="moe_gather",
    )(off, pos.reshape(nb, e, nt, MOE_TOK), hx)


def _ffn_kernel(xs_ref, wg_ref, wu_ref, wd_ref, y_ref, *, n_real):
    e, j = pl.program_id(0), pl.program_id(2)

    @pl.when(j < n_real)
    def _():
        x = xs_ref[0, 0, :, 0:D_MODEL]
        a = _dot(x, wg_ref[0])
        u = _dot(x, wu_ref[0])
        y = _dot((_silu(a) * u).astype(BF16), wd_ref[0])
        aff = xs_ref[0, 0, :, D_MODEL:HX_W].astype(F32)
        lane = _iota(aff.shape, 1)
        mine = (lane % N_EXPERTS == e) & (lane < 3 * N_EXPERTS)
        y_ref[0, 0] = y * jnp.sum(jnp.where(mine, aff, 0.0), axis=1, keepdims=True)

    @pl.when(j >= n_real)
    def _():
        y_ref[...] = jnp.zeros_like(y_ref)


def expert_ffn(xs, wg, wu, wd, cap, tm=256):
    nb, e, cpad, w = xs.shape
    d, f = wg.shape[1], wg.shape[2]
    n_real = -(-cap // tm)
    return pl.pallas_call(
        functools.partial(_ffn_kernel, n_real=n_real),
        grid=(e, nb, cpad // tm),
        in_specs=[pl.BlockSpec((1, 1, tm, w), lambda i, b, j: (b, i, jnp.minimum(j, n_real - 1), 0)),
                  pl.BlockSpec((1, d, f), lambda i, b, j: (i, 0, 0)),
                  pl.BlockSpec((1, d, f), lambda i, b, j: (i, 0, 0)),
                  pl.BlockSpec((1, f, d), lambda i, b, j: (i, 0, 0))],
        out_specs=pl.BlockSpec((1, 1, tm, d), lambda i, b, j: (b, i, j, 0)),
        out_shape=jax.ShapeDtypeStruct((nb, e, cpad, d), F32),
        compiler_params=_params("arbitrary", "arbitrary", "arbitrary"),
        name="expert_ffn",
    )(xs, wg, wu, wd)


def _combine_kernel(off_ref, pos_ref, y_ref, m_ref, *, nt):
    b, e = pl.program_id(0), pl.program_id(2)

    @pl.when(e == 0)
    def _():
        m_ref[...] = jnp.zeros_like(m_ref)

    def tile(j, carry):
        off, onehot = _slot_onehot(off_ref, pos_ref, b, e, j)
        slab = y_ref[0, 0, pl.ds(off, MOE_SLAB), :].astype(BF16)
        m_ref[0, pl.ds(pl.multiple_of(j * MOE_TOK, MOE_TOK), MOE_TOK), :] += _dot(onehot, slab, TN)
        return carry

    lax.fori_loop(0, nt, tile, 0)


def combine_tokens(y, pos, off, dq=256):
    nb, e, cpad, d = y.shape
    nt = off.shape[2]
    t_ = nt * MOE_TOK
    return pl.pallas_call(
        functools.partial(_combine_kernel, nt=nt),
        grid_spec=pltpu.PrefetchScalarGridSpec(
            num_scalar_prefetch=1,
            grid=(nb, d // dq, e),
            in_specs=[pl.BlockSpec((1, 1, nt, MOE_TOK), lambda b, q, e_, off_: (b, e_, 0, 0)),
                      pl.BlockSpec((1, 1, cpad, dq), lambda b, q, e_, off_: (b, e_, 0, q))],
            out_specs=pl.BlockSpec((1, t_, dq), lambda b, q, e_, off_: (b, 0, q))),
        out_shape=jax.ShapeDtypeStruct((nb, t_, d), F32),
        compiler_params=_params("arbitrary", "arbitrary", "arbitrary"),
        name="moe_combine",
    )(off, pos.reshape(nb, e, nt, MOE_TOK), y)


def ec_moe(hx, afft, t0, t_, wg, wu, wd):
    cap, _ = _cap_pad(t_)
    pos, off = select_tokens(afft, t0, t_)
    xs = gather_tokens(hx, pos, off, t0)
    y = expert_ffn(xs, wg, wu, wd, cap)
    return combine_tokens(y, pos, off)


def _residual_kernel(x_ref, m_ref, gate_ref, g_ref, o_ref):
    m = m_ref[0]
    ms = jnp.mean(m * m, axis=-1, keepdims=True)
    o_ref[0] = x_ref[0] + gate_ref[0] * (m * lax.rsqrt(ms + EPS) * g_ref[...])


def norm_residual(xall, m, t0, gate, g, tm=256):
    nb, ltot, d = xall.shape
    t_ = m.shape[1]
    per_b = t_ // tm
    nmod = gate.shape[0]
    mod_map = (lambda i: (i // per_b, 0, 0)) if nmod > 1 else (lambda i: (0, 0, 0))
    out = pl.pallas_call(
        _residual_kernel,
        grid=(nb * per_b,),
        in_specs=[pl.BlockSpec((1, tm, d), lambda i: (i // per_b, t0 // tm + i % per_b, 0)),
                  pl.BlockSpec((1, tm, d), lambda i: (i // per_b, i % per_b, 0)),
                  pl.BlockSpec((1, 1, d), mod_map),
                  _full(1, d)],
        out_specs=pl.BlockSpec((1, tm, d), lambda i: (i // per_b, i % per_b, 0)),
        out_shape=jax.ShapeDtypeStruct((nb, t_, d), F32),
        compiler_params=_params("arbitrary"),
        name="norm_residual",
    )(xall, m, gate.reshape(nmod, 1, d), g.reshape(1, d))
    return out


def _tri(reverse):
    r, c = _iota((CHUNK, CHUNK), 0), _iota((CHUNK, CHUNK), 1)
    return jnp.where((r <= c) if reverse else (r >= c), 1.0, 0.0).astype(F32)


def _block_mask(shape, rblk, cblk):
    return (_iota(shape, 0) // rblk) == (_iota(shape, 1) // cblk)


def _scan_tile(j, reverse, nt):
    return jnp.where(j == 0, nt - 1, (nt - 1 - j) if reverse else (j - 1))


def _tile_order(reverse, nt):
    return lambda j: (0, _scan_tile(j, reverse, nt), 0)


def _tile_index(reverse, nt):
    return _scan_tile(pl.program_id(0), reverse, nt)


def _halo_specs(nb, tile, w, reverse, nt):
    per = tile // HALO
    main = _tile_order(reverse, nt)
    last_blk = nt * per - 1

    def prev(j):
        return (0, jnp.maximum(main(j)[1] * per - 1, 0), 0)

    def nxt(j):
        return (0, jnp.minimum((main(j)[1] + 1) * per, last_blk), 0)

    return [pl.BlockSpec((nb, tile, w), main), pl.BlockSpec((nb, HALO, w), prev), pl.BlockSpec((nb, HALO, w), nxt)]


def _short_conv_tile(x, prev_row, next_row, w_ref, bias):
    t = x.shape[0]
    r = _iota(x.shape, 0)
    xm1 = jnp.where(r == 0, prev_row, pltpu.roll(x, 1, 0))
    xp1 = jnp.where(r == t - 1, next_row, pltpu.roll(x, t - 1, 0))
    y = xm1 * w_ref[0:1, :] + x * w_ref[1:2, :] + xp1 * w_ref[2:3, :]
    return y if bias is None else y + bias


def _conv_halo_ok(t, nt):
    return jnp.where((t >= 1) & (t <= nt - 2), 1.0, 0.0), jnp.where(t <= nt - 3, 1.0, 0.0)


def _scan_calls(kernel_fn, name, u_specs_fn, u_args, param_specs, params, scratch, nb, ltot, tile):
    nt = ltot // tile

    def call(reverse, extra):
        idx = _tile_order(reverse, nt)
        in_specs = u_specs_fn(reverse, nt) + list(param_specs)
        if reverse:
            in_specs.append(pl.BlockSpec((nb, tile, D_GROUP), idx))
        return pl.pallas_call(
            functools.partial(kernel_fn, reverse=reverse, nb=nb, nchunks=tile // CHUNK, nt=nt),
            grid=(nt,),
            in_specs=in_specs,
            out_specs=pl.BlockSpec((nb, tile, D_GROUP), idx),
            out_shape=jax.ShapeDtypeStruct((nb, ltot, D_GROUP), F32),
            scratch_shapes=scratch,
            compiler_params=_params("arbitrary"),
            name=name + ("_bwd" if reverse else "_fwd"),
        )(*u_args, *params, *extra)

    return call(True, (call(False, ()),))


def _gla_kernel(u_ref, w2_ref, b_ref, nw_ref, *rest, reverse, nb, nchunks, nt):
    if reverse:
        of_ref, o_ref, s_ref = rest
    else:
        o_ref, s_ref = rest
    nq = GLA_HEADS * GLA_DK

    @pl.when(pl.program_id(0) == 0)
    def _():
        s_ref[...] = jnp.zeros_like(s_ref)

    d = 1 if reverse else 0
    w2 = w2_ref[d]
    bias = b_ref[d]
    tri = _tri(reverse)
    r64, c256 = _iota((CHUNK, D_GROUP), 0), _iota((CHUNK, D_GROUP), 1) % CHUNK
    causal = (r64 <= c256) if reverse else (r64 >= c256)
    mask_k = _block_mask((D_GROUP, nq), CHUNK, GLA_DK)
    mask_v = _block_mask((D_GROUP, D_GROUP), CHUNK, GLA_DV)
    avg = jnp.where(mask_v, 1.0 / GLA_DV, 0.0).astype(F32)
    last = 0 if reverse else CHUNK - 1
    order = range(nchunks - 1, -1, -1) if reverse else range(nchunks)
    for c in order:
        rows = pl.ds(c * CHUNK, CHUNK)
        for b in range(nb):
            q = u_ref[b, rows, 0:nq] * (GLA_DK ** -0.5)
            k = u_ref[b, rows, nq:2 * nq]
            v = u_ref[b, rows, 2 * nq:2 * nq + D_GROUP]
            lr0 = 2 * nq + 2 * D_GROUP + d * GLA_RANK
            lr = u_ref[b, rows, lr0:lr0 + GLA_RANK]
            z = _dot(lr, w2, prec=HIGHEST) + bias
            g = jax.nn.log_sigmoid(z) / GLA_TAU
            cum = _dot(tri, g, prec=HIGHEST)
            cum_last = cum[last:last + 1, :]
            qg = (q * jnp.exp(cum)).astype(BF16)
            kg = k * jnp.exp(-cum)
            kend = (k * jnp.exp(cum_last - cum)).astype(BF16)
            dec = jnp.exp(cum_last)
            k_st = jnp.where(mask_k, jnp.concatenate([kg] * GLA_HEADS, axis=0), 0.0).astype(BF16)
            attn = jnp.where(causal, _dot(qg, k_st, NT), 0.0).astype(BF16)
            vb = v.astype(BF16)
            v_bd = jnp.where(mask_v, jnp.concatenate([vb] * GLA_HEADS, axis=0), jnp.zeros((), BF16))
            s_t = s_ref[b]
            o = _dot(attn, v_bd) + _dot(qg, s_t.astype(BF16), NT)
            s_ref[b] = s_t * dec + jnp.where(mask_k, _dot(vb, kend, TN), 0.0)
            if reverse:
                o = o + of_ref[b, rows, :]
                ms = _dot(o * o, avg, prec=HIGHEST)
                og = u_ref[b, rows, 2 * nq + D_GROUP:2 * nq + 2 * D_GROUP]
                o = o * lax.rsqrt(ms + EPS) * nw_ref[...] * _silu(og)
            o_ref[b, rows, :] = o


def gla_mixer(u, w2, b, norm_w, tile):
    nb, ltot, w = u.shape
    nq = GLA_HEADS * GLA_DK
    u_specs = lambda reverse, nt: [pl.BlockSpec((nb, tile, w), _tile_order(reverse, nt))]
    return _scan_calls(_gla_kernel, "gla", u_specs, (u,),
                       [_full(2, GLA_RANK, nq), _full(2, 1, nq), _full(1, D_GROUP)],
                       (w2, b.reshape(2, 1, nq), norm_w.reshape(1, D_GROUP)),
                       [pltpu.VMEM((nb, D_GROUP, nq), F32)], nb, ltot, tile)


def _ssd_kernel(u_ref, up_ref, un_ref, cw_ref, cb_ref, alog_ref, dtb_ref, dsk_ref, nw_ref, *rest,
                reverse, nb, nchunks, nt):
    if reverse:
        of_ref, o_ref, s_ref, xbc_ref = rest
    else:
        o_ref, s_ref, xbc_ref = rest
    t = _tile_index(reverse, nt)

    @pl.when(pl.program_id(0) == 0)
    def _():
        s_ref[...] = jnp.zeros_like(s_ref)

    prev_ok, next_ok = _conv_halo_ok(t, nt)
    c0, c1 = D_GROUP, D_GROUP + MB_CONV_CH
    for b in range(nb):
        x = u_ref[b, :, c0:c1]
        pr = up_ref[b, HALO - 1:HALO, c0:c1] * prev_ok
        nx = un_ref[b, 0:1, c0:c1] * next_ok
        xbc_ref[b] = _silu(_short_conv_tile(x, pr, nx, cw_ref, cb_ref[...]))

    d = 1 if reverse else 0
    a = -jnp.exp(alog_ref[d:d + 1, :])
    dtb = dtb_ref[d:d + 1, :]
    tri = _tri(reverse)
    r64, c256 = _iota((CHUNK, D_GROUP), 0), _iota((CHUNK, D_GROUP), 1) % CHUNK
    causal = (r64 <= c256) if reverse else (r64 >= c256)
    tri_t = jnp.where((r64 >= c256) if reverse else (r64 <= c256), 1.0, 0.0).astype(F32)
    e256 = jnp.where(_iota((MB_HEADS, D_GROUP), 0) == _iota((MB_HEADS, D_GROUP), 1) // CHUNK, 1.0, 0.0).astype(F32)
    hn = MB_HEADS * MB_STATE
    e512 = jnp.where(_iota((MB_HEADS, hn), 0) == _iota((MB_HEADS, hn), 1) // MB_STATE, 1.0, 0.0).astype(F32)
    gw = MB_GROUPS * MB_STATE
    rep = MB_HEADS // MB_GROUPS
    mask_b = (_iota((D_GROUP, gw), 0) // (CHUNK * rep)) == (_iota((D_GROUP, gw), 1) // MB_STATE)
    mask_v = _block_mask((D_GROUP, D_GROUP), CHUNK, MB_HEADDIM)
    mask_s = _block_mask((hn, D_GROUP), MB_STATE, MB_HEADDIM)
    avg = jnp.where(mask_v, 1.0 / MB_HEADDIM, 0.0).astype(F32)
    last = 0 if reverse else CHUNK - 1
    order = range(nchunks - 1, -1, -1) if reverse else range(nchunks)
    for c in order:
        rows = pl.ds(c * CHUNK, CHUNK)
        for b in range(nb):
            xs = xbc_ref[b, rows, 0:D_GROUP]
            bm = xbc_ref[b, rows, D_GROUP:D_GROUP + gw]
            cm = xbc_ref[b, rows, D_GROUP + gw:D_GROUP + 2 * gw]
            dt0 = D_GROUP + MB_CONV_CH + d * MB_HEADS
            dt = jax.nn.softplus(u_ref[b, rows, dt0:dt0 + MB_HEADS] + dtb)
            da = dt * a
            cum4 = _dot(tri, da, prec=HIGHEST)
            da_b = _dot(da, e256, prec=HIGHEST)
            cum_b = _dot(cum4, e256, prec=HIGHEST)
            cum_row = jnp.sum(da_b * tri_t, axis=0, keepdims=True)
            seg = jnp.exp(jnp.where(causal, cum_b - cum_row, -jnp.inf))
            cum512 = _dot(cum4, e512, prec=HIGHEST)
            cum_last512 = cum512[last:last + 1, :]
            xdt = xs * _dot(dt, e256, prec=HIGHEST)
            xdt_b = xdt.astype(BF16)
            b_st = jnp.where(mask_b, jnp.concatenate([bm] * MB_HEADS, axis=0), 0.0).astype(BF16)
            lmat = (_dot(cm.astype(BF16), b_st, NT) * seg).astype(BF16)
            xdt_bd = jnp.where(mask_v, jnp.concatenate([xdt_b] * MB_HEADS, axis=0), jnp.zeros((), BF16))
            c_cat = jnp.concatenate([cm[:, (h // rep) * MB_STATE:(h // rep + 1) * MB_STATE] for h in range(MB_HEADS)], 1)
            b_cat = jnp.concatenate([bm[:, (h // rep) * MB_STATE:(h // rep + 1) * MB_STATE] for h in range(MB_HEADS)], 1)
            c_cat = (c_cat * jnp.exp(cum512)).astype(BF16)
            b_cat = (b_cat * jnp.exp(cum_last512 - cum512)).astype(BF16)
            s = s_ref[b]
            y = _dot(lmat, xdt_bd) + _dot(c_cat, s.astype(BF16))
            dec = jnp.exp(cum_b[last:last + 1, :])
            s_ref[b] = s * dec + jnp.where(mask_s, _dot(b_cat, xdt_b, TN), 0.0)
            if reverse:
                y = y + of_ref[b, rows, :]
                yy = (y + xs * dsk_ref[...]) * _silu(u_ref[b, rows, 0:D_GROUP])
                ms = _dot(yy * yy, avg, prec=HIGHEST)
                y = yy * lax.rsqrt(ms + EPS) * nw_ref[...]
            o_ref[b, rows, :] = y


def mamba_mixer(u, conv_w, conv_b, a_log, dt_bias, d_skip, norm_w, tile):
    nb, ltot, w = u.shape
    d_cat = jnp.repeat(d_skip.astype(F32), MB_HEADDIM).reshape(1, D_GROUP)
    u_specs = lambda reverse, nt: _halo_specs(nb, tile, w, reverse, nt)
    return _scan_calls(_ssd_kernel, "ssd", u_specs, (u, u, u),
                       [_full(SHORT_CONV, MB_CONV_CH), _full(1, MB_CONV_CH), _full(2, MB_HEADS), _full(2, MB_HEADS),
                        _full(1, D_GROUP), _full(1, D_GROUP)],
                       (conv_w, conv_b.reshape(1, -1), a_log, dt_bias, d_cat, norm_w.reshape(1, D_GROUP)),
                       [pltpu.VMEM((nb, MB_HEADS * MB_STATE, D_GROUP), F32), pltpu.VMEM((nb, tile, MB_CONV_CH), F32)],
                       nb, ltot, tile)


def _to_bd(x_cat, mask):
    return jnp.where(mask, jnp.concatenate([x_cat] * DN_HEADS, axis=0), jnp.zeros((), x_cat.dtype))


def _dn_kernel(u_ref, up_ref, un_ref, cw_ref, alog_ref, dtb_ref, nw_ref, *rest, reverse, nb, nchunks, nt):
    if reverse:
        of_ref, o_ref, s_ref, qkv_ref = rest
    else:
        o_ref, s_ref, qkv_ref = rest
    t = _tile_index(reverse, nt)

    @pl.when(pl.program_id(0) == 0)
    def _():
        s_ref[...] = jnp.zeros_like(s_ref)

    prev_ok, next_ok = _conv_halo_ok(t, nt)
    for b in range(nb):
        x = u_ref[b, :, 0:DN_QKV]
        pr = up_ref[b, HALO - 1:HALO, 0:DN_QKV] * prev_ok
        nx = un_ref[b, 0:1, 0:DN_QKV] * next_ok
        qkv_ref[b] = _silu(_short_conv_tile(x, pr, nx, cw_ref, None))

    d = 1 if reverse else 0
    neg_a = -jnp.exp(alog_ref[d:d + 1, :])
    dtb = dtb_ref[d:d + 1, :]
    tri = _tri(reverse)
    r64, c256 = _iota((CHUNK, D_GROUP), 0), _iota((CHUNK, D_GROUP), 1) % CHUNK
    incl = (r64 <= c256) if reverse else (r64 >= c256)
    strict = (r64 < c256) if reverse else (r64 > c256)
    tri_t = jnp.where((r64 >= c256) if reverse else (r64 <= c256), 1.0, 0.0).astype(F32)
    eye_cat = jnp.where(r64 == c256, 1.0, 0.0).astype(F32)
    e256 = jnp.where(_iota((DN_HEADS, D_GROUP), 0) == _iota((DN_HEADS, D_GROUP), 1) // CHUNK, 1.0, 0.0).astype(F32)
    mask = _block_mask((D_GROUP, D_GROUP), CHUNK, DN_DK)
    ones_blk = jnp.where(mask, 1.0, 0.0).astype(F32)
    avg = ones_blk * (1.0 / DN_DV)
    last = 0 if reverse else CHUNK - 1
    order = range(nchunks - 1, -1, -1) if reverse else range(nchunks)
    for c in order:
        rows = pl.ds(c * CHUNK, CHUNK)
        for b in range(nb):
            q = qkv_ref[b, rows, 0:D_GROUP]
            k = qkv_ref[b, rows, D_GROUP:2 * D_GROUP]
            v = qkv_ref[b, rows, 2 * D_GROUP:3 * D_GROUP]
            qn = q * lax.rsqrt(_dot(q * q, ones_blk, prec=HIGHEST) + EPS) * (DN_DK ** -0.5)
            kn = k * lax.rsqrt(_dot(k * k, ones_blk, prec=HIGHEST) + EPS)
            b0 = 4 * D_GROUP + d * DN_HEADS
            a0 = 4 * D_GROUP + N_DIR * DN_HEADS + d * DN_HEADS
            beta = jax.nn.sigmoid(u_ref[b, rows, b0:b0 + DN_HEADS])
            g = neg_a * jax.nn.softplus(u_ref[b, rows, a0:a0 + DN_HEADS] + dtb)
            cum4 = _dot(tri, g, prec=HIGHEST)
            g_b = _dot(g, e256, prec=HIGHEST)
            cum_b = _dot(cum4, e256, prec=HIGHEST)
            beta_b = _dot(beta, e256, prec=HIGHEST)
            cum_row = jnp.sum(g_b * tri_t, axis=0, keepdims=True)
            decay = jnp.exp(jnp.where(incl, cum_b - cum_row, -jnp.inf))
            cum_last = cum_b[last:last + 1, :]
            e_cum = jnp.exp(cum_b)
            k_beta = kn * beta_b
            k_st = _to_bd(kn.astype(BF16), mask)
            m_cat = jnp.where(strict, _dot(k_beta.astype(BF16), k_st, NT) * decay, 0.0)
            attn = (_dot(qn.astype(BF16), k_st, NT) * decay).astype(BF16)
            t_cat = eye_cat - m_cat
            pw = _dot(m_cat.astype(BF16), _to_bd(m_cat.astype(BF16), mask))
            for _ in range(4):
                both = _dot(jnp.concatenate([t_cat, pw], axis=0).astype(BF16), _to_bd(pw.astype(BF16), mask))
                t_cat = t_cat + both[0:CHUNK]
                pw = both[CHUNK:2 * CHUNK]
            t_cat = t_cat + _dot(t_cat.astype(BF16), _to_bd(pw.astype(BF16), mask))
            rhs = jnp.concatenate([_to_bd((v * beta_b).astype(BF16), mask),
                                   _to_bd((k_beta * e_cum).astype(BF16), mask)], axis=1)
            sol = _dot(t_cat.astype(BF16), rhs)
            u_val, w_key = sol[:, 0:D_GROUP], sol[:, D_GROUP:2 * D_GROUP]
            s = s_ref[b]
            s_b = s.astype(BF16)
            v_new = u_val - _dot(w_key.astype(BF16), s_b)
            v_new_b = v_new.astype(BF16)
            o = _dot((qn * e_cum).astype(BF16), s_b) + _dot(attn, _to_bd(v_new_b, mask))
            k_end = (kn * jnp.exp(cum_last - cum_b)).astype(BF16)
            s_ref[b] = s * jnp.exp(cum_last) + jnp.where(mask, _dot(k_end, v_new_b, TN), 0.0)
            if reverse:
                o = o + of_ref[b, rows, :]
                ms = _dot(o * o, avg, prec=HIGHEST)
                og = u_ref[b, rows, 3 * D_GROUP:4 * D_GROUP]
                o = o * lax.rsqrt(ms + EPS) * nw_ref[...] * _silu(og)
            o_ref[b, rows, :] = o


def deltanet_mixer(u, conv_w, a_log, dt_bias, norm_w, tile):
    nb, ltot, w = u.shape
    u_specs = lambda reverse, nt: _halo_specs(nb, tile, w, reverse, nt)
    return _scan_calls(_dn_kernel, "dn", u_specs, (u, u, u),
                       [_full(SHORT_CONV, DN_QKV), _full(2, DN_HEADS), _full(2, DN_HEADS), _full(1, D_GROUP)],
                       (conv_w, a_log, dt_bias, norm_w.reshape(1, D_GROUP)),
                       [pltpu.VMEM((nb, D_GROUP, D_GROUP), F32), pltpu.VMEM((nb, tile, DN_QKV), F32)],
                       nb, ltot, tile)


def short_conv(u, w, b=None):
    k_, ch = w.shape
    y = lax.conv_general_dilated(u, w[:, None, :].astype(u.dtype), window_strides=(1,),
                                 padding=[(k_ // 2, k_ // 2)], dimension_numbers=('NWC', 'WIO', 'NWC'),
                                 feature_group_count=ch)
    if b is not None:
        y = y + b.astype(u.dtype)
    return y


def hyena_filters(l_, w1, b1, w2, b2, w3, b3, w4, freq):
    pos = jnp.arange(l_, dtype=F32)
    t = pos / max(l_ - 1, 1)
    bands = jnp.linspace(1e-4, HY_BANDS - 1, HY_BANDS, dtype=F32)
    ang = (2.0 * math.pi / l_) * pos[:, None] * bands[None, :]
    feats = jnp.concatenate([t[:, None], jnp.cos(ang), -jnp.sin(ang)], axis=-1)
    f = freq.astype(F32)
    h = jnp.sin(f[0] * (feats @ w1.astype(F32) + b1.astype(F32)))
    h = jnp.sin(f[1] * (h @ w2.astype(F32) + b2.astype(F32)))
    h = jnp.sin(f[2] * (h @ w3.astype(F32) + b3.astype(F32)))
    h = h @ w4.astype(F32)
    max_decay = math.log(HY_DECAY_TARGET) / HY_FAST_DECAY
    min_decay = math.log(HY_DECAY_TARGET) / HY_SLOW_DECAY
    deltas = jnp.abs(jnp.linspace(min_decay, max_decay, D_GROUP, dtype=F32))
    window = jnp.exp(-t[:, None] * deltas[None, :])
    return h.reshape(l_, HY_ORDER, N_DIR, D_GROUP) * window[:, None, None, :]


def long_conv(u, h_fwd, h_bwd, d_skip):
    l_, ch = h_fwd.shape
    kern = jnp.concatenate([h_fwd, jnp.zeros((1, ch), F32), jnp.flip(h_bwd[1:], 0)], axis=0)
    spec = jnp.fft.rfft(u, n=2 * l_, axis=1) * jnp.fft.rfft(kern, axis=0)[None]
    y = jnp.fft.irfft(spec, n=2 * l_, axis=1)[:, :l_]
    return y + u * d_skip


def hyena_mixer(u_ctx, u_lat, conv_w, conv_b, w1, b1, w2, b2, w3, b3, w4, freq, d_skip, need_ctx):
    d = d_skip.astype(F32)

    def run(u):
        l_ = u.shape[1]
        uc = short_conv(u, conv_w, conv_b).astype(F32)
        v, x1, x2 = jnp.split(uc, [D_GROUP, 2 * D_GROUP], axis=-1)
        h = hyena_filters(l_, w1, b1, w2, b2, w3, b3, w4, freq)
        z = x1 * long_conv(v, h[:, 0, 0], h[:, 0, 1], d[0])
        return (x2 * long_conv(z, h[:, 1, 0], h[:, 1, 1], d[1])).astype(u.dtype)

    return (run(u_ctx) if need_ctx else jnp.zeros(u_ctx.shape[:2] + (D_GROUP,), F32)), run(u_lat)


def to_col_major(u, rows):
    b_, l_, ch = u.shape
    return u.reshape(b_, rows, GRID_W, ch).transpose(0, 2, 1, 3).reshape(b_, l_, ch)


def to_row_major(u, rows):
    b_, l_, ch = u.shape
    return u.reshape(b_, GRID_W, rows, ch).transpose(0, 2, 1, 3).reshape(b_, l_, ch)


def kernel(x, c, ctx, c_ctx, w_mod, b_mod, norm_g, w_in, w_out, gla_w2, gla_b, gla_norm,
           hy_conv_w, hy_conv_b, hy_w1, hy_b1, hy_w2, hy_b2, hy_w3, hy_b3, hy_w4, hy_freq, hy_d,
           mb_conv_w, mb_conv_b, mb_a_log, mb_dt_bias, mb_d, mb_norm,
           dn_conv_w, dn_a_log, dn_dt_bias, dn_norm,
           moe_router, moe_w_gate, moe_w_up, moe_w_down):
    b_, seq, d_ = x.shape
    lc = ctx.shape[1]
    rows = seq // GRID_W
    xall = jnp.concatenate([x, ctx], axis=1)
    cond = jnp.concatenate([c, c_ctx[None, :], jnp.zeros((HALO - b_ - 1, d_), F32)], axis=0)
    for i in range(DEPTH):
        need_ctx = i < DEPTH - 1
        mod = modulation_all(cond, w_mod[i], b_mod[i])[:b_ + 1]
        sh1, sc1, g1, sh2, sc2, g2 = jnp.split(mod, N_MOD, axis=-1)
        w_out_b = w_out[i].astype(BF16)

        ua, ub, uc, ud = norm_mod_proj(xall, norm_g[i, 0], sh1, sc1, pad_in_proj(w_in[i]), lc)
        ya = gla_mixer(ua, gla_w2[i], gla_b[i], gla_norm[i], lc)
        yb_c, yb_l = hyena_mixer(ub[:, seq:], ub[:, :seq], hy_conv_w[i], hy_conv_b[i], hy_w1[i], hy_b1[i], hy_w2[i],
                                 hy_b2[i], hy_w3[i], hy_b3[i], hy_w4[i], hy_freq[i], hy_d[i], need_ctx)
        yb = jnp.concatenate([yb_l, yb_c], axis=1)
        uc = jnp.concatenate([to_col_major(uc[:, :seq], rows), uc[:, seq:]], axis=1)
        yc = mamba_mixer(uc, mb_conv_w[i], mb_conv_b[i], mb_a_log[i], mb_dt_bias[i], mb_d[i], mb_norm[i], lc)
        yc = jnp.concatenate([to_row_major(yc[:, :seq], rows), yc[:, seq:]], axis=1)
        yd = deltanet_mixer(ud, dn_conv_w[i], dn_a_log[i], dn_dt_bias[i], dn_norm[i], lc)
        xall = proj_norm_residual((ya, yb, yc, yd), xall, g1, norm_g[i, 1], w_out_b, lc)

        wg = moe_w_gate[i].astype(BF16)
        wu = moe_w_up[i].astype(BF16)
        wd = moe_w_down[i].astype(BF16)
        hx, afft = norm_mod_router(xall, norm_g[i, 2], sh2, sc2, moe_router[i], lc)
        x_l = norm_residual(xall, ec_moe(hx, afft, 0, seq, wg, wu, wd), 0, g2[:b_], norm_g[i, 3])
        if not need_ctx:
            return x_l
        x_c = norm_residual(xall, ec_moe(hx, afft, seq, lc, wg, wu, wd), seq, g2[b_:], norm_g[i, 3])
        xall = jnp.concatenate([x_l, x_c], axis=1)
    return xall[:, :seq]
```

```python
import math, functools
import numpy as np
import jax
import jax.numpy as jnp
from jax import lax
from jax.experimental import pallas as pl
from jax.experimental.pallas import tpu as pltpu

F32 = jnp.float32
BF16 = jnp.bfloat16

D_MODEL = 1024
DEPTH = 2
GRID_W = 64
N_MIXERS = 4
D_MIX = D_MODEL
D_GROUP = D_MIX // N_MIXERS
CHUNK = 64
SHORT_CONV = 3
N_DIR = 2
EPS = 1e-6
N_MOD = 6

GLA_HEADS = 4
GLA_DK = D_GROUP // (2 * GLA_HEADS)
GLA_DV = D_GROUP // GLA_HEADS
GLA_RANK = 16
GLA_TAU = 16.0

HY_ORDER = 2
HY_EMB = 33
HY_BANDS = (HY_EMB - 1) // 2
HY_FILTER_HIDDEN = 64
HY_DECAY_TARGET = 1e-2
HY_FAST_DECAY = 0.3
HY_SLOW_DECAY = 1.5

MB_HEADS = 4
MB_HEADDIM = D_GROUP // MB_HEADS
MB_STATE = 128
MB_GROUPS = 2
MB_CONV_CH = D_GROUP + 2 * MB_GROUPS * MB_STATE

DN_HEADS = 4
DN_DK = D_GROUP // DN_HEADS
DN_DV = D_GROUP // DN_HEADS
DN_QKV = 2 * DN_HEADS * DN_DK + DN_HEADS * DN_DV

N_EXPERTS = 16
EC_CAPACITY = 2
D_EXPERT = 1024

GLA_COLS = 2 * GLA_HEADS * GLA_DK + 2 * D_GROUP + N_DIR * GLA_RANK
HY_COLS = (HY_ORDER + 1) * D_GROUP
MB_COLS = D_GROUP + MB_CONV_CH + N_DIR * MB_HEADS
DN_COLS = 4 * D_GROUP + 2 * N_DIR * DN_HEADS
D_IN_PROJ = GLA_COLS + HY_COLS + MB_COLS + DN_COLS

LANES = 128
HALO = 8


def _lane_pad(n):
    return -(-n // LANES) * LANES


MOE_TOK = 256
MOE_SLAB = MOE_TOK + 16
HX_W = D_MODEL + LANES

MIX_COLS = (GLA_COLS, HY_COLS, MB_COLS, DN_COLS)
MIX_W = tuple(_lane_pad(n) for n in MIX_COLS)
GLA_W, HY_W, MB_W, DN_W = MIX_W

VMEM_LIMIT_BYTES = 48 * 1024 * 1024
HIGHEST = lax.Precision.HIGHEST

NN = (((1,), (0,)), ((), ()))
NT = (((1,), (1,)), ((), ()))
TN = (((0,), (0,)), ((), ()))


def _params(*sem):
    return pltpu.CompilerParams(dimension_semantics=sem, vmem_limit_bytes=VMEM_LIMIT_BYTES)


def _dot(a, b, dims=NN, prec=None):
    return lax.dot_general(a, b, dims, precision=prec, preferred_element_type=F32)


def _iota(shape, axis):
    return lax.broadcasted_iota(jnp.int32, shape, axis)


def _silu(x):
    return x * jax.nn.sigmoid(x)


def _full(*shape):
    return pl.BlockSpec(shape, lambda *_: (0,) * len(shape))


def _modulation_kernel(c_ref, w_ref, b_ref, o_ref):
    o_ref[...] = _dot(_silu(c_ref[...]), w_ref[...], prec=HIGHEST) + b_ref[...]


def modulation_all(cond, w_mod, b_mod):
    r, d = cond.shape
    n = w_mod.shape[1]
    tn = 1024
    return pl.pallas_call(
        _modulation_kernel,
        grid=(n // tn,),
        in_specs=[pl.BlockSpec((r, d), lambda j: (0, 0)),
                  pl.BlockSpec((d, tn), lambda j: (0, j)),
                  pl.BlockSpec((1, tn), lambda j: (0, j))],
        out_specs=pl.BlockSpec((r, tn), lambda j: (0, j)),
        out_shape=jax.ShapeDtypeStruct((r, n), F32),
        compiler_params=_params("arbitrary"),
        name="modulation",
    )(cond, w_mod, b_mod.reshape(1, n))


def _norm_mod(x, g, sh, sc):
    ms = jnp.mean(x * x, axis=-1, keepdims=True)
    y = x * lax.rsqrt(ms + EPS) * g
    return y * (1.0 + sc) + sh


def _mod_map(per_b, ctx_tiles, nb):
    return lambda i: (jnp.where(i % per_b >= per_b - ctx_tiles, nb, i // per_b), 0, 0)


def _inproj_kernel(x_ref, sh_ref, sc_ref, g_ref, w_ref, *o_refs, tn):
    yb = _norm_mod(x_ref[...], g_ref[...], sh_ref[0], sc_ref[0]).astype(BF16)
    col = 0
    for o_ref in o_refs:
        n = o_ref.shape[1]
        for j in range(0, n, tn):
            w = min(tn, n - j)
            o_ref[:, j:j + w] = _dot(yb, w_ref[:, col + j:col + j + w])
        col += n


def norm_mod_proj(xall, g, shift, scale, w_pad, ctx_len, tm=256, tn=512):
    nb, ltot, d = xall.shape
    per_b = ltot // tm
    mod_map = _mod_map(per_b, ctx_len // tm, nb)
    outs = pl.pallas_call(
        functools.partial(_inproj_kernel, tn=tn),
        grid=(nb * per_b,),
        in_specs=[pl.BlockSpec((tm, d), lambda i: (i, 0)),
                  pl.BlockSpec((1, 1, d), mod_map),
                  pl.BlockSpec((1, 1, d), mod_map),
                  _full(1, d),
                  _full(d, w_pad.shape[1])],
        out_specs=[pl.BlockSpec((tm, w), lambda i: (i, 0)) for w in MIX_W],
        out_shape=[jax.ShapeDtypeStruct((nb * ltot, w), F32) for w in MIX_W],
        compiler_params=_params("arbitrary"),
        name="norm_mod_proj",
    )(xall.reshape(nb * ltot, d), shift.reshape(nb + 1, 1, d), scale.reshape(nb + 1, 1, d), g.reshape(1, d), w_pad)
    return [o.reshape(nb, ltot, w) for o, w in zip(outs, MIX_W)]


def pad_in_proj(w_in):
    cols, c0 = [], 0
    for n, w in zip(MIX_COLS, MIX_W):
        cols.append(jnp.pad(w_in[:, c0:c0 + n], ((0, 0), (0, w - n))))
        c0 += n
    return jnp.concatenate(cols, axis=1).astype(BF16)


def _outproj_kernel(ya_ref, yb_ref, yc_ref, yd_ref, x_ref, gate_ref, g_ref, w_ref, o_ref):
    p = None
    for m, y_ref in enumerate((ya_ref, yb_ref, yc_ref, yd_ref)):
        t = _dot(y_ref[...].astype(BF16), w_ref[m * D_GROUP:(m + 1) * D_GROUP, :])
        p = t if p is None else p + t
    ms = jnp.mean(p * p, axis=-1, keepdims=True)
    o_ref[...] = x_ref[...] + gate_ref[0] * (p * lax.rsqrt(ms + EPS) * g_ref[...])


def proj_norm_residual(ys, xall, gate, g, w_bf16, ctx_len, tm=256):
    nb, ltot, d = xall.shape
    per_b = ltot // tm
    mod_map = _mod_map(per_b, ctx_len // tm, nb)
    row = lambda w: pl.BlockSpec((tm, w), lambda i: (i, 0))
    out = pl.pallas_call(
        _outproj_kernel,
        grid=(nb * per_b,),
        in_specs=[row(D_GROUP)] * N_MIXERS + [row(d), pl.BlockSpec((1, 1, d), mod_map), _full(1, d), _full(D_MIX, d)],
        out_specs=row(d),
        out_shape=jax.ShapeDtypeStruct((nb * ltot, d), F32),
        compiler_params=_params("arbitrary"),
        name="proj_norm_residual",
    )(*[y.reshape(nb * ltot, D_GROUP) for y in ys], xall.reshape(nb * ltot, d), gate.reshape(nb + 1, 1, d),
      g.reshape(1, d), w_bf16)
    return out.reshape(nb, ltot, d)


def _router_kernel(x_ref, sh_ref, sc_ref, g_ref, r_ref, rt_ref, hx_ref, afft_ref):
    h = _norm_mod(x_ref[...], g_ref[...], sh_ref[0], sc_ref[0])
    hx_ref[:, 0:D_MODEL] = h.astype(BF16)
    logits = _dot(h, r_ref[...], prec=HIGHEST)
    e = jnp.exp(logits - jnp.max(logits, axis=-1, keepdims=True))
    aff = e / jnp.sum(e, axis=-1, keepdims=True)
    a1 = aff.astype(BF16)
    r1 = aff - a1.astype(F32)
    a2 = r1.astype(BF16)
    a3 = (r1 - a2.astype(F32)).astype(BF16)
    pad = jnp.zeros((aff.shape[0], LANES - 3 * N_EXPERTS), BF16)
    hx_ref[:, D_MODEL:HX_W] = jnp.concatenate([a1, a2, a3, pad], axis=1)
    lt = _dot(rt_ref[...], h, NT, prec=HIGHEST)
    et = jnp.exp(lt - jnp.max(lt, axis=0, keepdims=True))
    afft_ref[0] = et / jnp.sum(et, axis=0, keepdims=True)


def norm_mod_router(xall, g, shift, scale, router, ctx_len, tm=256):
    nb, ltot, d = xall.shape
    e = router.shape[1]
    per_b = ltot // tm
    mod_map = _mod_map(per_b, ctx_len // tm, nb)
    hx, afft = pl.pallas_call(
        _router_kernel,
        grid=(nb * per_b,),
        in_specs=[pl.BlockSpec((tm, d), lambda i: (i, 0)),
                  pl.BlockSpec((1, 1, d), mod_map),
                  pl.BlockSpec((1, 1, d), mod_map),
                  _full(1, d), _full(d, e), _full(e, d)],
        out_specs=[pl.BlockSpec((tm, HX_W), lambda i: (i, 0)),
                   pl.BlockSpec((1, e, tm), lambda i: (i // per_b, 0, i % per_b))],
        out_shape=[jax.ShapeDtypeStruct((nb * ltot, HX_W), BF16),
                   jax.ShapeDtypeStruct((nb, e, ltot), F32)],
        compiler_params=_params("arbitrary"),
        name="norm_mod_router",
    )(xall.reshape(nb * ltot, d), shift.reshape(nb + 1, 1, d), scale.reshape(nb + 1, 1, d), g.reshape(1, d),
      router, router.T)
    return hx.reshape(nb, ltot, HX_W), afft


def _cap_pad(t_):
    cap = EC_CAPACITY * t_ // N_EXPERTS
    return cap, -(-(cap + MOE_SLAB) // MOE_TOK) * MOE_TOK


def _lane_prefix(x, t_):
    upper = jnp.where(_iota((LANES, LANES), 0) <= _iota((LANES, LANES), 1), 1.0, 0.0).astype(BF16)
    carry = jnp.zeros((x.shape[0], 1), F32)
    out = []
    for j in range(t_ // LANES):
        p = _dot(x[:, j * LANES:(j + 1) * LANES].astype(BF16), upper) + carry
        out.append(p)
        carry = p[:, LANES - 1:LANES]
    return jnp.concatenate(out, axis=1)


def _select_kernel(afft_ref, pos_ref, off_ref, *, cap, t_):
    key = pltpu.bitcast(afft_ref[0], jnp.int32)
    thr = jnp.zeros((key.shape[0], 1), jnp.int32)
    for bit in range(30, -1, -1):
        cand = thr | (1 << bit)
        cnt = jnp.sum(jnp.where(key >= cand, 1, 0), axis=1, keepdims=True)
        thr = jnp.where(cnt >= cap, cand, thr)
    gt = key > thr
    eq = key == thr
    need = (cap - jnp.sum(jnp.where(gt, 1, 0), axis=1, keepdims=True)).astype(F32)
    eq_rank = _lane_prefix(jnp.where(eq, 1.0, 0.0), t_)
    sel = gt | (eq & (eq_rank <= need))
    sel_f = jnp.where(sel, 1.0, 0.0)
    slot = _lane_prefix(sel_f, t_) - sel_f
    pos_ref[0] = jnp.where(sel, slot, -1.0).astype(jnp.int32)
    nt = t_ // MOE_TOK
    before = jnp.where(_iota((t_, LANES), 0) < _iota((t_, LANES), 1) * MOE_TOK, 1.0, 0.0).astype(BF16)
    off_ref[0] = _dot(sel_f.astype(BF16), before)[:, 0:nt].astype(jnp.int32)


def select_tokens(afft, t0, t_):
    nb, e, _ = afft.shape
    cap, _ = _cap_pad(t_)
    nt = t_ // MOE_TOK
    return pl.pallas_call(
        functools.partial(_select_kernel, cap=cap, t_=t_),
        grid=(nb,),
        in_specs=[pl.BlockSpec((1, e, t_), lambda b: (b, 0, t0 // t_))],
        out_specs=[pl.BlockSpec((1, e, t_), lambda b: (b, 0, 0)), pl.BlockSpec((1, e, nt), lambda b: (b, 0, 0))],
        out_shape=[jax.ShapeDtypeStruct((nb, e, t_), jnp.int32), jax.ShapeDtypeStruct((nb, e, nt), jnp.int32)],
        compiler_params=_params("arbitrary"),
        name="moe_select",
    )(afft)


def _slot_onehot(off_ref, pos_ref, b, e, j):
    off = pl.multiple_of((off_ref[b, e, j] // 16) * 16, 16)
    pos = pos_ref[0, 0, pl.ds(j, 1), :]
    r = _iota((MOE_SLAB, MOE_TOK), 0)
    return off, jnp.where(pos - off == r, 1.0, 0.0).astype(BF16)


def _gather_kernel(off_ref, pos_ref, hx_ref, xs_ref, acc_ref, *, nt):
    b, e = pl.program_id(0), pl.program_id(1)
    acc_ref[...] = jnp.zeros_like(acc_ref)

    def tile(j, carry):
        off, onehot = _slot_onehot(off_ref, pos_ref, b, e, j)
        rows = hx_ref[0, pl.ds(pl.multiple_of(j * MOE_TOK, MOE_TOK), MOE_TOK), :]
        acc_ref[pl.ds(off, MOE_SLAB), :] += _dot(onehot, rows)
        return carry

    lax.fori_loop(0, nt, tile, 0)
    xs_ref[0, 0] = acc_ref[...].astype(BF16)


def gather_tokens(hx, pos, off, t0):
    nb, _, w = hx.shape
    e, nt = off.shape[1], off.shape[2]
    t_ = nt * MOE_TOK
    _, cpad = _cap_pad(t_)
    return pl.pallas_call(
        functools.partial(_gather_kernel, nt=nt),
        grid_spec=pltpu.PrefetchScalarGridSpec(
            num_scalar_prefetch=1,
            grid=(nb, e),
            in_specs=[pl.BlockSpec((1, 1, nt, MOE_TOK), lambda b, e_, off_: (b, e_, 0, 0)),
                      pl.BlockSpec((1, t_, w), lambda b, e_, off_: (b, t0 // t_, 0), pipeline_mode=pl.Buffered(1))],
            out_specs=pl.BlockSpec((1, 1, cpad, w), lambda b, e_, off_: (b, e_, 0, 0)),
            scratch_shapes=[pltpu.VMEM((cpad, w), F32)]),
        out_shape=jax.ShapeDtypeStruct((nb, e, cpad, w), BF16),
        compiler_params=_params("arbitrary", "arbitrary"),
        name="moe_gather",
    )(off, pos.reshape(nb, e, nt, MOE_TOK), hx)


def _ffn_kernel(xs_ref, wg_ref, wu_ref, wd_ref, y_ref, *, n_real):
    e, j = pl.program_id(0), pl.program_id(2)

    @pl.when(j < n_real)
    def _():
        x = xs_ref[0, 0, :, 0:D_MODEL]
        a = _dot(x, wg_ref[0])
        u = _dot(x, wu_ref[0])
        y = _dot((_silu(a) * u).astype(BF16), wd_ref[0])
        aff = xs_ref[0, 0, :, D_MODEL:HX_W].astype(F32)
        lane = _iota(aff.shape, 1)
        mine = (lane % N_EXPERTS == e) & (lane < 3 * N_EXPERTS)
        y_ref[0, 0] = y * jnp.sum(jnp.where(mine, aff, 0.0), axis=1, keepdims=True)

    @pl.when(j >= n_real)
    def _():
        y_ref[...] = jnp.zeros_like(y_ref)


def expert_ffn(xs, wg, wu, wd, cap, tm=256):
    nb, e, cpad, w = xs.shape
    d, f = wg.shape[1], wg.shape[2]
    n_real = -(-cap // tm)
    return pl.pallas_call(
        functools.partial(_ffn_kernel, n_real=n_real),
        grid=(e, nb, cpad // tm),
        in_specs=[pl.BlockSpec((1, 1, tm, w), lambda i, b, j: (b, i, jnp.minimum(j, n_real - 1), 0)),
                  pl.BlockSpec((1, d, f), lambda i, b, j: (i, 0, 0)),
                  pl.BlockSpec((1, d, f), lambda i, b, j: (i, 0, 0)),
                  pl.BlockSpec((1, f, d), lambda i, b, j: (i, 0, 0))],
        out_specs=pl.BlockSpec((1, 1, tm, d), lambda i, b, j: (b, i, j, 0)),
        out_shape=jax.ShapeDtypeStruct((nb, e, cpad, d), F32),
        compiler_params=_params("arbitrary", "arbitrary", "arbitrary"),
        name="expert_ffn",
    )(xs, wg, wu, wd)


def _combine_kernel(off_ref, pos_ref, y_ref, m_ref, *, nt):
    b, e = pl.program_id(0), pl.program_id(2)

    @pl.when(e == 0)
    def _():
        m_ref[...] = jnp.zeros_like(m_ref)

    def tile(j, carry):
        off, onehot = _slot_onehot(off_ref, pos_ref, b, e, j)
        slab = y_ref[0, 0, pl.ds(off, MOE_SLAB), :].astype(BF16)
        m_ref[0, pl.ds(pl.multiple_of(j * MOE_TOK, MOE_TOK), MOE_TOK), :] += _dot(onehot, slab, TN)
        return carry

    lax.fori_loop(0, nt, tile, 0)


def combine_tokens(y, pos, off, dq=256):
    nb, e, cpad, d = y.shape
    nt = off.shape[2]
    t_ = nt * MOE_TOK
    return pl.pallas_call(
        functools.partial(_combine_kernel, nt=nt),
        grid_spec=pltpu.PrefetchScalarGridSpec(
            num_scalar_prefetch=1,
            grid=(nb, d // dq, e),
            in_specs=[pl.BlockSpec((1, 1, nt, MOE_TOK), lambda b, q, e_, off_: (b, e_, 0, 0)),
                      pl.BlockSpec((1, 1, cpad, dq), lambda b, q, e_, off_: (b, e_, 0, q))],
            out_specs=pl.BlockSpec((1, t_, dq), lambda b, q, e_, off_: (b, 0, q))),
        out_shape=jax.ShapeDtypeStruct((nb, t_, d), F32),
        compiler_params=_params("arbitrary", "arbitrary", "arbitrary"),
        name="moe_combine",
    )(off, pos.reshape(nb, e, nt, MOE_TOK), y)


def ec_moe(hx, afft, t0, t_, wg, wu, wd):
    cap, _ = _cap_pad(t_)
    pos, off = select_tokens(afft, t0, t_)
    xs = gather_tokens(hx, pos, off, t0)
    y = expert_ffn(xs, wg, wu, wd, cap)
    return combine_tokens(y, pos, off)


def _residual_kernel(x_ref, m_ref, gate_ref, g_ref, o_ref):
    m = m_ref[0]
    ms = jnp.mean(m * m, axis=-1, keepdims=True)
    o_ref[0] = x_ref[0] + gate_ref[0] * (m * lax.rsqrt(ms + EPS) * g_ref[...])


def norm_residual(xall, m, t0, gate, g, tm=256):
    nb, ltot, d = xall.shape
    t_ = m.shape[1]
    per_b = t_ // tm
    nmod = gate.shape[0]
    mod_map = (lambda i: (i // per_b, 0, 0)) if nmod > 1 else (lambda i: (0, 0, 0))
    out = pl.pallas_call(
        _residual_kernel,
        grid=(nb * per_b,),
        in_specs=[pl.BlockSpec((1, tm, d), lambda i: (i // per_b, t0 // tm + i % per_b, 0)),
                  pl.BlockSpec((1, tm, d), lambda i: (i // per_b, i % per_b, 0)),
                  pl.BlockSpec((1, 1, d), mod_map),
                  _full(1, d)],
        out_specs=pl.BlockSpec((1, tm, d), lambda i: (i // per_b, i % per_b, 0)),
        out_shape=jax.ShapeDtypeStruct((nb, t_, d), F32),
        compiler_params=_params("arbitrary"),
        name="norm_residual",
    )(xall, m, gate.reshape(nmod, 1, d), g.reshape(1, d))
    return out


def _tri(reverse):
    r, c = _iota((CHUNK, CHUNK), 0), _iota((CHUNK, CHUNK), 1)
    return jnp.where((r <= c) if reverse else (r >= c), 1.0, 0.0).astype(F32)


def _block_mask(shape, rblk, cblk):
    return (_iota(shape, 0) // rblk) == (_iota(shape, 1) // cblk)


def _scan_tile(j, reverse, nt):
    return jnp.where(j == 0, nt - 1, (nt - 1 - j) if reverse else (j - 1))


def _tile_order(reverse, nt):
    return lambda j: (0, _scan_tile(j, reverse, nt), 0)


def _tile_index(reverse, nt):
    return _scan_tile(pl.program_id(0), reverse, nt)


def _halo_specs(nb, tile, w, reverse, nt):
    per = tile // HALO
    main = _tile_order(reverse, nt)
    last_blk = nt * per - 1

    def prev(j):
        return (0, jnp.maximum(main(j)[1] * per - 1, 0), 0)

    def nxt(j):
        return (0, jnp.minimum((main(j)[1] + 1) * per, last_blk), 0)

    return [pl.BlockSpec((nb, tile, w), main), pl.BlockSpec((nb, HALO, w), prev), pl.BlockSpec((nb, HALO, w), nxt)]


def _short_conv_tile(x, prev_row, next_row, w_ref, bias):
    t = x.shape[0]
    r = _iota(x.shape, 0)
    xm1 = jnp.where(r == 0, prev_row, pltpu.roll(x, 1, 0))
    xp1 = jnp.where(r == t - 1, next_row, pltpu.roll(x, t - 1, 0))
    y = xm1 * w_ref[0:1, :] + x * w_ref[1:2, :] + xp1 * w_ref[2:3, :]
    return y if bias is None else y + bias


def _conv_halo_ok(t, nt):
    return jnp.where((t >= 1) & (t <= nt - 2), 1.0, 0.0), jnp.where(t <= nt - 3, 1.0, 0.0)


def _scan_calls(kernel_fn, name, u_specs_fn, u_args, param_specs, params, scratch, nb, ltot, tile):
    nt = ltot // tile

    def call(reverse, extra):
        idx = _tile_order(reverse, nt)
        in_specs = u_specs_fn(reverse, nt) + list(param_specs)
        if reverse:
            in_specs.append(pl.BlockSpec((nb, tile, D_GROUP), idx))
        return pl.pallas_call(
            functools.partial(kernel_fn, reverse=reverse, nb=nb, nchunks=tile // CHUNK, nt=nt),
            grid=(nt,),
            in_specs=in_specs,
            out_specs=pl.BlockSpec((nb, tile, D_GROUP), idx),
            out_shape=jax.ShapeDtypeStruct((nb, ltot, D_GROUP), F32),
            scratch_shapes=scratch,
            compiler_params=_params("arbitrary"),
            name=name + ("_bwd" if reverse else "_fwd"),
        )(*u_args, *params, *extra)

    return call(True, (call(False, ()),))


def _gla_kernel(u_ref, w2_ref, b_ref, nw_ref, *rest, reverse, nb, nchunks, nt):
    if reverse:
        of_ref, o_ref, s_ref = rest
    else:
        o_ref, s_ref = rest
    nq = GLA_HEADS * GLA_DK

    @pl.when(pl.program_id(0) == 0)
    def _():
        s_ref[...] = jnp.zeros_like(s_ref)

    d = 1 if reverse else 0
    w2 = w2_ref[d]
    bias = b_ref[d]
    tri = _tri(reverse)
    r64, c256 = _iota((CHUNK, D_GROUP), 0), _iota((CHUNK, D_GROUP), 1) % CHUNK
    causal = (r64 <= c256) if reverse else (r64 >= c256)
    mask_k = _block_mask((D_GROUP, nq), CHUNK, GLA_DK)
    mask_v = _block_mask((D_GROUP, D_GROUP), CHUNK, GLA_DV)
    avg = jnp.where(mask_v, 1.0 / GLA_DV, 0.0).astype(F32)
    last = 0 if reverse else CHUNK - 1
    order = range(nchunks - 1, -1, -1) if reverse else range(nchunks)
    for c in order:
        rows = pl.ds(c * CHUNK, CHUNK)
        for b in range(nb):
            q = u_ref[b, rows, 0:nq] * (GLA_DK ** -0.5)
            k = u_ref[b, rows, nq:2 * nq]
            v = u_ref[b, rows, 2 * nq:2 * nq + D_GROUP]
            lr0 = 2 * nq + 2 * D_GROUP + d * GLA_RANK
            lr = u_ref[b, rows, lr0:lr0 + GLA_RANK]
            z = _dot(lr, w2, prec=HIGHEST) + bias
            g = jax.nn.log_sigmoid(z) / GLA_TAU
            cum = _dot(tri, g, prec=HIGHEST)
            cum_last = cum[last:last + 1, :]
            qg = (q * jnp.exp(cum)).astype(BF16)
            kg = k * jnp.exp(-cum)
            kend = (k * jnp.exp(cum_last - cum)).astype(BF16)
            dec = jnp.exp(cum_last)
            k_st = jnp.where(mask_k, jnp.concatenate([kg] * GLA_HEADS, axis=0), 0.0).astype(BF16)
            attn = jnp.where(causal, _dot(qg, k_st, NT), 0.0).astype(BF16)
            vb = v.astype(BF16)
            v_bd = jnp.where(mask_v, jnp.concatenate([vb] * GLA_HEADS, axis=0), jnp.zeros((), BF16))
            s_t = s_ref[b]
            o = _dot(attn, v_bd) + _dot(qg, s_t.astype(BF16), NT)
            s_ref[b] = s_t * dec + jnp.where(mask_k, _dot(vb, kend, TN), 0.0)
            if reverse:
                o = o + of_ref[b, rows, :]
                ms = _dot(o * o, avg, prec=HIGHEST)
                og = u_ref[b, rows, 2 * nq + D_GROUP:2 * nq + 2 * D_GROUP]
                o = o * lax.rsqrt(ms + EPS) * nw_ref[...] * _silu(og)
            o_ref[b, rows, :] = o


def gla_mixer(u, w2, b, norm_w, tile):
    nb, ltot, w = u.shape
    nq = GLA_HEADS * GLA_DK
    u_specs = lambda reverse, nt: [pl.BlockSpec((nb, tile, w), _tile_order(reverse, nt))]
    return _scan_calls(_gla_kernel, "gla", u_specs, (u,),
                       [_full(2, GLA_RANK, nq), _full(2, 1, nq), _full(1, D_GROUP)],
                       (w2, b.reshape(2, 1, nq), norm_w.reshape(1, D_GROUP)),
                       [pltpu.VMEM((nb, D_GROUP, nq), F32)], nb, ltot, tile)


def _ssd_kernel(u_ref, up_ref, un_ref, cw_ref, cb_ref, alog_ref, dtb_ref, dsk_ref, nw_ref, *rest,
                reverse, nb, nchunks, nt):
    if reverse:
        of_ref, o_ref, s_ref, xbc_ref = rest
    else:
        o_ref, s_ref, xbc_ref = rest
    t = _tile_index(reverse, nt)

    @pl.when(pl.program_id(0) == 0)
    def _():
        s_ref[...] = jnp.zeros_like(s_ref)

    prev_ok, next_ok = _conv_halo_ok(t, nt)
    c0, c1 = D_GROUP, D_GROUP + MB_CONV_CH
    for b in range(nb):
        x = u_ref[b, :, c0:c1]
        pr = up_ref[b, HALO - 1:HALO, c0:c1] * prev_ok
        nx = un_ref[b, 0:1, c0:c1] * next_ok
        xbc_ref[b] = _silu(_short_conv_tile(x, pr, nx, cw_ref, cb_ref[...]))

    d = 1 if reverse else 0
    a = -jnp.exp(alog_ref[d:d + 1, :])
    dtb = dtb_ref[d:d + 1, :]
    tri = _tri(reverse)
    r64, c256 = _iota((CHUNK, D_GROUP), 0), _iota((CHUNK, D_GROUP), 1) % CHUNK
    causal = (r64 <= c256) if reverse else (r64 >= c256)
    tri_t = jnp.where((r64 >= c256) if reverse else (r64 <= c256), 1.0, 0.0).astype(F32)
    e256 = jnp.where(_iota((MB_HEADS, D_GROUP), 0) == _iota((MB_HEADS, D_GROUP), 1) // CHUNK, 1.0, 0.0).astype(F32)
    hn = MB_HEADS * MB_STATE
    e512 = jnp.where(_iota((MB_HEADS, hn), 0) == _iota((MB_HEADS, hn), 1) // MB_STATE, 1.0, 0.0).astype(F32)
    gw = MB_GROUPS * MB_STATE
    rep = MB_HEADS // MB_GROUPS
    mask_b = (_iota((D_GROUP, gw), 0) // (CHUNK * rep)) == (_iota((D_GROUP, gw), 1) // MB_STATE)
    mask_v = _block_mask((D_GROUP, D_GROUP), CHUNK, MB_HEADDIM)
    mask_s = _block_mask((hn, D_GROUP), MB_STATE, MB_HEADDIM)
    avg = jnp.where(mask_v, 1.0 / MB_HEADDIM, 0.0).astype(F32)
    last = 0 if reverse else CHUNK - 1
    order = range(nchunks - 1, -1, -1) if reverse else range(nchunks)
    for c in order:
        rows = pl.ds(c * CHUNK, CHUNK)
        for b in range(nb):
            xs = xbc_ref[b, rows, 0:D_GROUP]
            bm = xbc_ref[b, rows, D_GROUP:D_GROUP + gw]
            cm = xbc_ref[b, rows, D_GROUP + gw:D_GROUP + 2 * gw]
            dt0 = D_GROUP + MB_CONV_CH + d * MB_HEADS
            dt = jax.nn.softplus(u_ref[b, rows, dt0:dt0 + MB_HEADS] + dtb)
            da = dt * a
            cum4 = _dot(tri, da, prec=HIGHEST)
            da_b = _dot(da, e256, prec=HIGHEST)
            cum_b = _dot(cum4, e256, prec=HIGHEST)
            cum_row = jnp.sum(da_b * tri_t, axis=0, keepdims=True)
            seg = jnp.exp(jnp.where(causal, cum_b - cum_row, -jnp.inf))
            cum512 = _dot(cum4, e512, prec=HIGHEST)
            cum_last512 = cum512[last:last + 1, :]
            xdt = xs * _dot(dt, e256, prec=HIGHEST)
            xdt_b = xdt.astype(BF16)
            b_st = jnp.where(mask_b, jnp.concatenate([bm] * MB_HEADS, axis=0), 0.0).astype(BF16)
            lmat = (_dot(cm.astype(BF16), b_st, NT) * seg).astype(BF16)
            xdt_bd = jnp.where(mask_v, jnp.concatenate([xdt_b] * MB_HEADS, axis=0), jnp.zeros((), BF16))
            c_cat = jnp.concatenate([cm[:, (h // rep) * MB_STATE:(h // rep + 1) * MB_STATE] for h in range(MB_HEADS)], 1)
            b_cat = jnp.concatenate([bm[:, (h // rep) * MB_STATE:(h // rep + 1) * MB_STATE] for h in range(MB_HEADS)], 1)
            c_cat = (c_cat * jnp.exp(cum512)).astype(BF16)
            b_cat = (b_cat * jnp.exp(cum_last512 - cum512)).astype(BF16)
            s = s_ref[b]
            y = _dot(lmat, xdt_bd) + _dot(c_cat, s.astype(BF16))
            dec = jnp.exp(cum_b[last:last + 1, :])
            s_ref[b] = s * dec + jnp.where(mask_s, _dot(b_cat, xdt_b, TN), 0.0)
            if reverse:
                y = y + of_ref[b, rows, :]
                yy = (y + xs * dsk_ref[...]) * _silu(u_ref[b, rows, 0:D_GROUP])
                ms = _dot(yy * yy, avg, prec=HIGHEST)
                y = yy * lax.rsqrt(ms + EPS) * nw_ref[...]
            o_ref[b, rows, :] = y


def mamba_mixer(u, conv_w, conv_b, a_log, dt_bias, d_skip, norm_w, tile):
    nb, ltot, w = u.shape
    d_cat = jnp.repeat(d_skip.astype(F32), MB_HEADDIM).reshape(1, D_GROUP)
    u_specs = lambda reverse, nt: _halo_specs(nb, tile, w, reverse, nt)
    return _scan_calls(_ssd_kernel, "ssd", u_specs, (u, u, u),
                       [_full(SHORT_CONV, MB_CONV_CH), _full(1, MB_CONV_CH), _full(2, MB_HEADS), _full(2, MB_HEADS),
                        _full(1, D_GROUP), _full(1, D_GROUP)],
                       (conv_w, conv_b.reshape(1, -1), a_log, dt_bias, d_cat, norm_w.reshape(1, D_GROUP)),
                       [pltpu.VMEM((nb, MB_HEADS * MB_STATE, D_GROUP), F32), pltpu.VMEM((nb, tile, MB_CONV_CH), F32)],
                       nb, ltot, tile)


def _to_bd(x_cat, mask):
    return jnp.where(mask, jnp.concatenate([x_cat] * DN_HEADS, axis=0), jnp.zeros((), x_cat.dtype))


def _dn_kernel(u_ref, up_ref, un_ref, cw_ref, alog_ref, dtb_ref, nw_ref, *rest, reverse, nb, nchunks, nt):
    if reverse:
        of_ref, o_ref, s_ref, qkv_ref = rest
    else:
        o_ref, s_ref, qkv_ref = rest
    t = _tile_index(reverse, nt)

    @pl.when(pl.program_id(0) == 0)
    def _():
        s_ref[...] = jnp.zeros_like(s_ref)

    prev_ok, next_ok = _conv_halo_ok(t, nt)
    for b in range(nb):
        x = u_ref[b, :, 0:DN_QKV]
        pr = up_ref[b, HALO - 1:HALO, 0:DN_QKV] * prev_ok
        nx = un_ref[b, 0:1, 0:DN_QKV] * next_ok
        qkv_ref[b] = _silu(_short_conv_tile(x, pr, nx, cw_ref, None))

    d = 1 if reverse else 0
    neg_a = -jnp.exp(alog_ref[d:d + 1, :])
    dtb = dtb_ref[d:d + 1, :]
    tri = _tri(reverse)
    r64, c256 = _iota((CHUNK, D_GROUP), 0), _iota((CHUNK, D_GROUP), 1) % CHUNK
    incl = (r64 <= c256) if reverse else (r64 >= c256)
    strict = (r64 < c256) if reverse else (r64 > c256)
    tri_t = jnp.where((r64 >= c256) if reverse else (r64 <= c256), 1.0, 0.0).astype(F32)
    eye_cat = jnp.where(r64 == c256, 1.0, 0.0).astype(F32)
    same_blk = [(r64 // s) == (c256 // s) for s in (8, 16, 32, 64)]
    e256 = jnp.where(_iota((DN_HEADS, D_GROUP), 0) == _iota((DN_HEADS, D_GROUP), 1) // CHUNK, 1.0, 0.0).astype(F32)
    mask = _block_mask((D_GROUP, D_GROUP), CHUNK, DN_DK)
    ones_blk = jnp.where(mask, 1.0, 0.0).astype(F32)
    avg = ones_blk * (1.0 / DN_DV)
    last = 0 if reverse else CHUNK - 1
    order = range(nchunks - 1, -1, -1) if reverse else range(nchunks)
    for c in order:
        rows = pl.ds(c * CHUNK, CHUNK)
        for b in range(nb):
            q = qkv_ref[b, rows, 0:D_GROUP]
            k = qkv_ref[b, rows, D_GROUP:2 * D_GROUP]
            v = qkv_ref[b, rows, 2 * D_GROUP:3 * D_GROUP]
            qn = q * lax.rsqrt(_dot(q * q, ones_blk, prec=HIGHEST) + EPS) * (DN_DK ** -0.5)
            kn = k * lax.rsqrt(_dot(k * k, ones_blk, prec=HIGHEST) + EPS)
            b0 = 4 * D_GROUP + d * DN_HEADS
            a0 = 4 * D_GROUP + N_DIR * DN_HEADS + d * DN_HEADS
            beta = jax.nn.sigmoid(u_ref[b, rows, b0:b0 + DN_HEADS])
            g = neg_a * jax.nn.softplus(u_ref[b, rows, a0:a0 + DN_HEADS] + dtb)
            cum4 = _dot(tri, g, prec=HIGHEST)
            g_b = _dot(g, e256, prec=HIGHEST)
            cum_b = _dot(cum4, e256, prec=HIGHEST)
            beta_b = _dot(beta, e256, prec=HIGHEST)
            cum_row = jnp.sum(g_b * tri_t, axis=0, keepdims=True)
            decay = jnp.exp(jnp.where(incl, cum_b - cum_row, -jnp.inf))
            cum_last = cum_b[last:last + 1, :]
            e_cum = jnp.exp(cum_b)
            k_beta = kn * beta_b
            k_st = _to_bd(kn.astype(BF16), mask)
            m_cat = jnp.where(strict, _dot(k_beta.astype(BF16), k_st, NT) * decay, 0.0)
            attn = (_dot(qn.astype(BF16), k_st, NT) * decay).astype(BF16)
            m8 = jnp.where(same_blk[0], m_cat, 0.0).astype(BF16)
            t_cat = eye_cat - m8
            pw = _dot(m8, _to_bd(m8, mask))
            both = _dot(jnp.concatenate([t_cat, pw], axis=0).astype(BF16), _to_bd(pw.astype(BF16), mask))
            t_cat = t_cat + both[0:CHUNK]
            t_cat = t_cat + _dot(t_cat.astype(BF16), _to_bd(both[CHUNK:2 * CHUNK].astype(BF16), mask))
            for lvl in range(1, len(same_blk)):
                m_off = jnp.where(same_blk[lvl] & jnp.logical_not(same_blk[lvl - 1]), m_cat, 0.0).astype(BF16)
                tm_off = _dot(t_cat.astype(BF16), _to_bd(m_off, mask))
                t_cat = t_cat - _dot(tm_off.astype(BF16), _to_bd(t_cat.astype(BF16), mask))
            rhs = jnp.concatenate([_to_bd((v * beta_b).astype(BF16), mask),
                                   _to_bd((k_beta * e_cum).astype(BF16), mask)], axis=1)
            sol = _dot(t_cat.astype(BF16), rhs)
            u_val, w_key = sol[:, 0:D_GROUP], sol[:, D_GROUP:2 * D_GROUP]
            s = s_ref[b]
            s_b = s.astype(BF16)
            v_new = u_val - _dot(w_key.astype(BF16), s_b)
            v_new_b = v_new.astype(BF16)
            o = _dot((qn * e_cum).astype(BF16), s_b) + _dot(attn, _to_bd(v_new_b, mask))
            k_end = (kn * jnp.exp(cum_last - cum_b)).astype(BF16)
            s_ref[b] = s * jnp.exp(cum_last) + jnp.where(mask, _dot(k_end, v_new_b, TN), 0.0)
            if reverse:
                o = o + of_ref[b, rows, :]
                ms = _dot(o * o, avg, prec=HIGHEST)
                og = u_ref[b, rows, 3 * D_GROUP:4 * D_GROUP]
                o = o * lax.rsqrt(ms + EPS) * nw_ref[...] * _silu(og)
            o_ref[b, rows, :] = o


def deltanet_mixer(u, conv_w, a_log, dt_bias, norm_w, tile):
    nb, ltot, w = u.shape
    u_specs = lambda reverse, nt: _halo_specs(nb, tile, w, reverse, nt)
    return _scan_calls(_dn_kernel, "dn", u_specs, (u, u, u),
                       [_full(SHORT_CONV, DN_QKV), _full(2, DN_HEADS), _full(2, DN_HEADS), _full(1, D_GROUP)],
                       (conv_w, a_log, dt_bias, norm_w.reshape(1, D_GROUP)),
                       [pltpu.VMEM((nb, D_GROUP, D_GROUP), F32), pltpu.VMEM((nb, tile, DN_QKV), F32)],
                       nb, ltot, tile)


HY_CB = 8
HY_FROWS = 8


def _dft_constants(nr):
    k = np.arange(nr)[:, None] * np.arange(nr)[None, :]
    a, b = np.cos(2 * np.pi * k / nr), -np.sin(2 * np.pi * k / nr)
    kl = np.arange(LANES)[:, None] * np.arange(LANES)[None, :]
    cr, ci = np.cos(2 * np.pi * kl / LANES), -np.sin(2 * np.pi * kl / LANES)
    kt = np.arange(nr)[:, None] * np.arange(LANES)[None, :]
    tr, ti = np.cos(2 * np.pi * kt / (nr * LANES)), -np.sin(2 * np.pi * kt / (nr * LANES))
    h = nr // 2
    return dict(
        ab_half=jnp.asarray(np.concatenate([a[:, :h], b[:, :h]], 0), BF16),
        ab_full=jnp.asarray(np.concatenate([a, b], 0), BF16),
        atbt=jnp.asarray(np.concatenate([a[:, :h].T, b[:, :h].T], 0), BF16),
        cc=jnp.asarray(np.concatenate([cr, ci], 1), BF16),
        tw=jnp.asarray(np.stack([tr, ti]), F32))


def _rows_fwd(ab, xs, nr):
    rhs = jnp.concatenate([jnp.concatenate([re, im], axis=1) for re, im in xs], axis=1).astype(BF16)
    out = _dot(ab, rhs)
    res = []
    for i in range(len(xs)):
        blk = out[:, 2 * LANES * i:2 * LANES * (i + 1)]
        res.append((blk[0:nr, 0:LANES] - blk[nr:2 * nr, LANES:2 * LANES],
                    blk[0:nr, LANES:2 * LANES] + blk[nr:2 * nr, 0:LANES]))
    return res


def _lanes_dft(cc, zs, nr, conj):
    lhs = jnp.concatenate([t for z in zs for t in z], axis=0).astype(BF16)
    out = _dot(lhs, cc)
    res = []
    for i in range(len(zs)):
        rr = out[2 * nr * i:2 * nr * i + nr, 0:LANES]
        ri = out[2 * nr * i:2 * nr * i + nr, LANES:2 * LANES]
        ir = out[2 * nr * i + nr:2 * nr * (i + 1), 0:LANES]
        ii = out[2 * nr * i + nr:2 * nr * (i + 1), LANES:2 * LANES]
        res.append((rr + ii, ir - ri) if conj else (rr - ii, ri + ir))
    return res


def _rows_inv(atbt, vs, nr):
    h = nr // 2
    rhs = jnp.concatenate([jnp.concatenate([re, im], axis=1) for re, im in vs], axis=1).astype(BF16)
    out = _dot(atbt, rhs)
    res = []
    for i in range(len(vs)):
        blk = out[:, 2 * LANES * i:2 * LANES * (i + 1)]
        res.append((blk[0:h, 0:LANES] + blk[h:nr, LANES:2 * LANES], blk[0:h, LANES:2 * LANES] - blk[h:nr, 0:LANES]))
    return res


def _cmul(x, y, conj=False):
    (xr, xi), (yr, yi) = x, y
    if conj:
        return xr * yr + xi * yi, xi * yr - xr * yi
    return xr * yr - xi * yi, xr * yi + xi * yr


def _conv_pairs(pairs, spec, consts, nr):
    ab, atbt, cc, tw = consts
    y = _rows_fwd(ab, pairs, nr)
    z = [_cmul(t, tw) for t in y]
    x = _lanes_dft(cc, z, nr, False)
    g = [_cmul(t, spec) for t in x]
    u = _lanes_dft(cc, g, nr, True)
    v = [_cmul(t, tw, conj=True) for t in u]
    scale = 1.0 / (nr * LANES)
    return [(re * scale, im * scale) for re, im in _rows_inv(atbt, v, nr)]


def _shift_conv(x, w0, w1, w2, bias):
    rows, lane = _iota(x.shape, 0), _iota(x.shape, 1)
    r1 = pltpu.roll(x, 1, 1)
    xm1 = jnp.where(lane == 0, jnp.where(rows == 0, 0.0, pltpu.roll(r1, 1, 0)), r1)
    l1 = pltpu.roll(x, LANES - 1, 1)
    xp1 = jnp.where(lane == LANES - 1, jnp.where(rows == x.shape[0] - 1, 0.0, pltpu.roll(l1, x.shape[0] - 1, 0)), l1)
    return xm1 * w0 + x * w1 + xp1 * w2 + bias


def _hy_conv_kernel(p_ref, v_ref, x1_ref, x2_ref, k_ref, abh_ref, abf_ref, atbt_ref, cc_ref, tw_ref, o_ref,
                    *, nseq, nr, ctx_rows):
    g, jc = pl.program_id(0), pl.program_id(1)
    h = nr // 2
    consts = (abh_ref[...], atbt_ref[...], cc_ref[...], (tw_ref[0], tw_ref[1]))
    ab_full = abf_ref[...]
    valid = _iota((h, LANES), 0) < jnp.where(g == 0, h, ctx_rows)
    zero = jnp.zeros((nr, LANES), F32)
    for i in range(HY_CB):
        c = jc * HY_CB + i
        parts = []
        for part, ref in enumerate((v_ref, x1_ref, x2_ref)):
            ch = part * D_GROUP + c
            w0, w1, w2, bias = p_ref[0, ch], p_ref[1, ch], p_ref[2, ch], p_ref[3, ch]
            parts.append([jnp.where(valid, _shift_conv(ref[s, i], w0, w1, w2, bias), 0.0) for s in range(nseq)])
        v, x1, x2 = parts
        kf = _rows_fwd(ab_full, [(k_ref[0, o, i], zero) for o in range(HY_ORDER)], nr)
        kf = _lanes_dft(consts[2], [_cmul(t, consts[3]) for t in kf], nr, False)
        cur = v
        for o in range(HY_ORDER):
            d = p_ref[4 + o, c]
            conv = _conv_pairs([(cur[s], cur[s + 1]) for s in range(0, nseq, 2)], kf[o], consts, nr)
            conv = [t for pair in conv for t in pair]
            gate = x1 if o == 0 else x2
            cur = [gate[s] * (conv[s] + cur[s] * d) for s in range(nseq)]
        for s in range(nseq):
            o_ref[s, i] = cur[s]


def hyena_conv(seqs, kern, conv_w, conv_b, d_skip, ngroups, ctx_rows):
    ns, ch, h, _ = seqs.shape
    nseq = ns // ngroups
    nr = 2 * h
    cst = _dft_constants(nr)
    p = jnp.zeros((8, ch), F32).at[0:3].set(conv_w).at[3].set(conv_b).at[4:6, 0:D_GROUP].set(d_skip)
    ncb = D_GROUP // HY_CB
    seq_spec = lambda part: pl.BlockSpec((nseq, HY_CB, h, LANES), lambda g, j: (g, part * ncb + j, 0, 0))
    return pl.pallas_call(
        functools.partial(_hy_conv_kernel, nseq=nseq, nr=nr, ctx_rows=ctx_rows),
        grid=(ngroups, ncb),
        in_specs=[pl.BlockSpec(memory_space=pltpu.SMEM), seq_spec(0), seq_spec(1), seq_spec(2),
                  pl.BlockSpec((1, HY_ORDER, HY_CB, nr, LANES), lambda g, j: (g, 0, j, 0, 0)),
                  _full(2 * nr, h), _full(2 * nr, nr), _full(nr, nr), _full(LANES, 2 * LANES), _full(2, nr, LANES)],
        out_specs=pl.BlockSpec((nseq, HY_CB, h, LANES), lambda g, j: (g, j, 0, 0)),
        out_shape=jax.ShapeDtypeStruct((ns, D_GROUP, h, LANES), F32),
        compiler_params=_params("arbitrary", "arbitrary"),
        name="hyena_conv",
    )(p, seqs, seqs, seqs, kern, cst["ab_half"], cst["ab_full"], cst["atbt"], cst["cc"], cst["tw"])


def _hy_filter_kernel(len_ref, w1t_ref, b1_ref, w2t_ref, b2_ref, w3t_ref, b3_ref, w4t_ref, f_ref, bands_ref,
                      deltas_ref, o_ref, *, nr):
    g, jb = pl.program_id(0), pl.program_id(1)
    l_ = len_ref[g]
    lf = l_.astype(F32)
    n = nr * LANES
    w = HY_FROWS * LANES
    tau = (jb * w + _iota((1, w), 1))
    backward = tau > n // 2
    posi = jnp.where(backward, n - tau, tau)
    ok = jnp.where(backward, jnp.where(posi < l_, 1.0, 0.0), jnp.where(tau < l_, 1.0, 0.0))
    pos = posi.astype(F32)
    t = pos / jnp.maximum(lf - 1.0, 1.0)
    ang = (2.0 * math.pi / lf) * pos * bands_ref[...]
    w1t = w1t_ref[...]
    f = f_ref[...]
    hid = (w1t[:, 0:1] * t + _dot(w1t[:, 1:1 + HY_BANDS], jnp.cos(ang), prec=HIGHEST)
           + _dot(w1t[:, 1 + HY_BANDS:HY_EMB], -jnp.sin(ang), prec=HIGHEST))
    hid = jnp.sin(f[:, 0:1] * (hid + b1_ref[...]))
    hid = jnp.sin(f[:, 1:2] * (_dot(w2t_ref[...], hid, prec=HIGHEST) + b2_ref[...]))
    hid = jnp.sin(f[:, 2:3] * (_dot(w3t_ref[...], hid, prec=HIGHEST) + b3_ref[...]))
    out = _dot(w4t_ref[0], hid, prec=HIGHEST)
    window = jnp.exp(-t * deltas_ref[...]) * ok
    out = out * jnp.concatenate([window] * HY_ORDER, axis=0)
    for r in range(HY_FROWS):
        o_ref[0, r] = out[:, r * LANES:(r + 1) * LANES]


def hyena_filter_kernels(lens, w1, b1, w2, b2, w3, b3, w4, freq, nr):
    ng = lens.shape[0]
    hf = HY_FILTER_HIDDEN
    oc = HY_ORDER * D_GROUP
    w4t = w4.reshape(hf, HY_ORDER, N_DIR, D_GROUP).transpose(2, 1, 3, 0).reshape(N_DIR, oc, hf)
    bands = jnp.linspace(1e-4, HY_BANDS - 1, HY_BANDS, dtype=F32).reshape(HY_BANDS, 1)
    max_decay = math.log(HY_DECAY_TARGET) / HY_FAST_DECAY
    min_decay = math.log(HY_DECAY_TARGET) / HY_SLOW_DECAY
    deltas = jnp.abs(jnp.linspace(min_decay, max_decay, D_GROUP, dtype=F32)).reshape(D_GROUP, 1)
    nblk = nr // HY_FROWS
    col = lambda v: v.reshape(-1, 1)
    return pl.pallas_call(
        functools.partial(_hy_filter_kernel, nr=nr),
        grid=(ng, nblk),
        in_specs=[pl.BlockSpec(memory_space=pltpu.SMEM),
                  _full(hf, HY_EMB), _full(hf, 1), _full(hf, hf), _full(hf, 1), _full(hf, hf), _full(hf, 1),
                  pl.BlockSpec((1, oc, hf), lambda g, j: ((2 * j) // nblk, 0, 0)),
                  _full(hf, 3), _full(HY_BANDS, 1), _full(D_GROUP, 1)],
        out_specs=pl.BlockSpec((1, HY_FROWS, oc, LANES), lambda g, j: (g, j, 0, 0)),
        out_shape=jax.ShapeDtypeStruct((ng, nr, oc, LANES), F32),
        compiler_params=_params("arbitrary", "arbitrary"),
        name="hyena_filters",
    )(lens, w1.T, col(b1), w2.T, col(b2), w3.T, col(b3), w4t, freq.T, bands, deltas)


def hyena_mixer(u_lat, u_ctx, conv_w, conv_b, w1, b1, w2, b2, w3, b3, w4, freq, d_skip, need_ctx):
    nb, l_, ch = u_lat.shape
    lc = u_ctx.shape[1]
    h = l_ // LANES
    nr = 2 * h
    to_tiles = lambda u: u.transpose(0, 2, 1).reshape(u.shape[0], ch, -1, LANES)
    seqs = to_tiles(u_lat)
    lens = [l_]
    if need_ctx:
        seqs = jnp.concatenate([seqs, to_tiles(jnp.pad(u_ctx, ((0, 0), (0, l_ - lc), (0, 0))))], axis=0)
        lens.append(lc)
    ng = len(lens)
    kern = hyena_filter_kernels(jnp.asarray(lens, jnp.int32), w1, b1, w2, b2, w3, b3, w4, freq, nr)
    kern = kern.reshape(ng, nr, HY_ORDER, D_GROUP, LANES).transpose(0, 2, 3, 1, 4)
    y = hyena_conv(seqs, kern, conv_w, conv_b, d_skip, ng, lc // LANES)
    from_tiles = lambda t: t.reshape(t.shape[0], D_GROUP, -1).transpose(0, 2, 1)
    y_lat = from_tiles(y[:nb])
    y_ctx = from_tiles(y[nb:])[:, :lc] if need_ctx else jnp.zeros((nb, lc, D_GROUP), F32)
    return y_lat, y_ctx


def to_col_major(u, rows):
    b_, l_, ch = u.shape
    return u.reshape(b_, rows, GRID_W, ch).transpose(0, 2, 1, 3).reshape(b_, l_, ch)


def to_row_major(u, rows):
    b_, l_, ch = u.shape
    return u.reshape(b_, GRID_W, rows, ch).transpose(0, 2, 1, 3).reshape(b_, l_, ch)


def kernel(x, c, ctx, c_ctx, w_mod, b_mod, norm_g, w_in, w_out, gla_w2, gla_b, gla_norm,
           hy_conv_w, hy_conv_b, hy_w1, hy_b1, hy_w2, hy_b2, hy_w3, hy_b3, hy_w4, hy_freq, hy_d,
           mb_conv_w, mb_conv_b, mb_a_log, mb_dt_bias, mb_d, mb_norm,
           dn_conv_w, dn_a_log, dn_dt_bias, dn_norm,
           moe_router, moe_w_gate, moe_w_up, moe_w_down):
    b_, seq, d_ = x.shape
    lc = ctx.shape[1]
    rows = seq // GRID_W
    xall = jnp.concatenate([x, ctx], axis=1)
    cond = jnp.concatenate([c, c_ctx[None, :], jnp.zeros((HALO - b_ - 1, d_), F32)], axis=0)
    for i in range(DEPTH):
        need_ctx = i < DEPTH - 1
        mod = modulation_all(cond, w_mod[i], b_mod[i])[:b_ + 1]
        sh1, sc1, g1, sh2, sc2, g2 = jnp.split(mod, N_MOD, axis=-1)
        w_out_b = w_out[i].astype(BF16)

        ua, ub, uc, ud = norm_mod_proj(xall, norm_g[i, 0], sh1, sc1, pad_in_proj(w_in[i]), lc)
        ya = gla_mixer(ua, gla_w2[i], gla_b[i], gla_norm[i], lc)
        yb_l, yb_c = hyena_mixer(ub[:, :seq], ub[:, seq:], hy_conv_w[i], hy_conv_b[i], hy_w1[i], hy_b1[i], hy_w2[i],
                                 hy_b2[i], hy_w3[i], hy_b3[i], hy_w4[i], hy_freq[i], hy_d[i], need_ctx)
        yb = jnp.concatenate([yb_l, yb_c], axis=1)
        uc = jnp.concatenate([to_col_major(uc[:, :seq], rows), uc[:, seq:]], axis=1)
        yc = mamba_mixer(uc, mb_conv_w[i], mb_conv_b[i], mb_a_log[i], mb_dt_bias[i], mb_d[i], mb_norm[i], lc)
        yc = jnp.concatenate([to_row_major(yc[:, :seq], rows), yc[:, seq:]], axis=1)
        yd = deltanet_mixer(ud, dn_conv_w[i], dn_a_log[i], dn_dt_bias[i], dn_norm[i], lc)
        xall = proj_norm_residual((ya, yb, yc, yd), xall, g1, norm_g[i, 1], w_out_b, lc)

        wg = moe_w_gate[i].astype(BF16)
        wu = moe_w_up[i].astype(BF16)
        wd = moe_w_down[i].astype(BF16)
        hx, afft = norm_mod_router(xall, norm_g[i, 2], sh2, sc2, moe_router[i], lc)
        x_l = norm_residual(xall, ec_moe(hx, afft, 0, seq, wg, wu, wd), 0, g2[:b_], norm_g[i, 3])
        if not need_ctx:
            return x_l
        x_c = norm_residual(xall, ec_moe(hx, afft, seq, lc, wg, wu, wd), seq, g2[b_:], norm_g[i, 3])
        xall = jnp.concatenate([x_l, x_c], axis=1)
    return xall[:, :seq]
```

```python
import math, functools
import numpy as np
import jax
import jax.numpy as jnp
from jax import lax
from jax.experimental import pallas as pl
from jax.experimental.pallas import tpu as pltpu

F32 = jnp.float32
BF16 = jnp.bfloat16

D_MODEL = 1024
DEPTH = 2
GRID_W = 64
N_MIXERS = 4
D_MIX = D_MODEL
D_GROUP = D_MIX // N_MIXERS
CHUNK = 64
SHORT_CONV = 3
N_DIR = 2
EPS = 1e-6
N_MOD = 6

GLA_HEADS = 4
GLA_DK = D_GROUP // (2 * GLA_HEADS)
GLA_DV = D_GROUP // GLA_HEADS
GLA_RANK = 16
GLA_TAU = 16.0

HY_ORDER = 2
HY_EMB = 33
HY_BANDS = (HY_EMB - 1) // 2
HY_FILTER_HIDDEN = 64
HY_DECAY_TARGET = 1e-2
HY_FAST_DECAY = 0.3
HY_SLOW_DECAY = 1.5

MB_HEADS = 4
MB_HEADDIM = D_GROUP // MB_HEADS
MB_STATE = 128
MB_GROUPS = 2
MB_CONV_CH = D_GROUP + 2 * MB_GROUPS * MB_STATE

DN_HEADS = 4
DN_DK = D_GROUP // DN_HEADS
DN_DV = D_GROUP // DN_HEADS
DN_QKV = 2 * DN_HEADS * DN_DK + DN_HEADS * DN_DV

N_EXPERTS = 16
EC_CAPACITY = 2
D_EXPERT = 1024

GLA_COLS = 2 * GLA_HEADS * GLA_DK + 2 * D_GROUP + N_DIR * GLA_RANK
HY_COLS = (HY_ORDER + 1) * D_GROUP
MB_COLS = D_GROUP + MB_CONV_CH + N_DIR * MB_HEADS
DN_COLS = 4 * D_GROUP + 2 * N_DIR * DN_HEADS
D_IN_PROJ = GLA_COLS + HY_COLS + MB_COLS + DN_COLS

LANES = 128
HALO = 8


def _lane_pad(n):
    return -(-n // LANES) * LANES


MOE_TOK = 256
MOE_WIN = 128
SUBLANES_BF16 = 16
FFN_TM = 512
GATHER_E = 2
HX_W = D_MODEL + LANES

MIX_COLS = (GLA_COLS, HY_COLS, MB_COLS, DN_COLS)
MIX_W = tuple(_lane_pad(n) for n in MIX_COLS)
GLA_W, HY_W, MB_W, DN_W = MIX_W

VMEM_LIMIT_BYTES = 48 * 1024 * 1024
HIGHEST = lax.Precision.HIGHEST

NN = (((1,), (0,)), ((), ()))
NT = (((1,), (1,)), ((), ()))
TN = (((0,), (0,)), ((), ()))


def _params(*sem):
    return pltpu.CompilerParams(dimension_semantics=sem, vmem_limit_bytes=VMEM_LIMIT_BYTES)


def _dot(a, b, dims=NN, prec=None):
    return lax.dot_general(a, b, dims, precision=prec, preferred_element_type=F32)


def _iota(shape, axis):
    return lax.broadcasted_iota(jnp.int32, shape, axis)


def _silu(x):
    return x * jax.nn.sigmoid(x)


def _full(*shape):
    return pl.BlockSpec(shape, lambda *_: (0,) * len(shape))


def _modulation_kernel(c_ref, w_ref, b_ref, o_ref):
    o_ref[...] = _dot(_silu(c_ref[...]), w_ref[...], prec=HIGHEST) + b_ref[...]


def modulation_all(cond, w_mod, b_mod):
    r, d = cond.shape
    n = w_mod.shape[1]
    tn = 1024
    return pl.pallas_call(
        _modulation_kernel,
        grid=(n // tn,),
        in_specs=[pl.BlockSpec((r, d), lambda j: (0, 0)),
                  pl.BlockSpec((d, tn), lambda j: (0, j)),
                  pl.BlockSpec((1, tn), lambda j: (0, j))],
        out_specs=pl.BlockSpec((r, tn), lambda j: (0, j)),
        out_shape=jax.ShapeDtypeStruct((r, n), F32),
        compiler_params=_params("arbitrary"),
        name="modulation",
    )(cond, w_mod, b_mod.reshape(1, n))


def _norm_mod(x, g, sh, sc):
    ms = jnp.mean(x * x, axis=-1, keepdims=True)
    y = x * lax.rsqrt(ms + EPS) * g
    return y * (1.0 + sc) + sh


def _mod_map(per_b, ctx_tiles, nb):
    return lambda i: (jnp.where(i % per_b >= per_b - ctx_tiles, nb, i // per_b), 0, 0)


def _inproj_kernel(x_ref, sh_ref, sc_ref, g_ref, w_ref, wt_ref, ot_ref, *o_refs, tn):
    yb = _norm_mod(x_ref[...], g_ref[...], sh_ref[0], sc_ref[0]).astype(BF16)
    ot_ref[0] = _dot(wt_ref[...], yb, NT)
    col = 0
    for o_ref in o_refs:
        n = o_ref.shape[1]
        for j in range(0, n, tn):
            w = min(tn, n - j)
            o_ref[:, j:j + w] = _dot(yb, w_ref[:, col + j:col + j + w])
        col += n


def norm_mod_proj(xall, g, shift, scale, w_rows, w_hy_t, ctx_len, tm=256, tn=512):
    nb, ltot, d = xall.shape
    per_b = ltot // tm
    mod_map = _mod_map(per_b, ctx_len // tm, nb)
    row_w = (GLA_W, MB_W, DN_W)
    outs = pl.pallas_call(
        functools.partial(_inproj_kernel, tn=tn),
        grid=(nb * per_b,),
        in_specs=[pl.BlockSpec((tm, d), lambda i: (i, 0)),
                  pl.BlockSpec((1, 1, d), mod_map),
                  pl.BlockSpec((1, 1, d), mod_map),
                  _full(1, d),
                  _full(d, w_rows.shape[1]),
                  _full(HY_COLS, d)],
        out_specs=[pl.BlockSpec((1, HY_COLS, tm), lambda i: (i // per_b, 0, i % per_b))]
        + [pl.BlockSpec((tm, w), lambda i: (i, 0)) for w in row_w],
        out_shape=[jax.ShapeDtypeStruct((nb, HY_COLS, ltot), F32)]
        + [jax.ShapeDtypeStruct((nb * ltot, w), F32) for w in row_w],
        compiler_params=_params("arbitrary"),
        name="norm_mod_proj",
    )(xall.reshape(nb * ltot, d), shift.reshape(nb + 1, 1, d), scale.reshape(nb + 1, 1, d), g.reshape(1, d),
      w_rows, w_hy_t)
    ua, uc, ud = [o.reshape(nb, ltot, w) for o, w in zip(outs[1:], row_w)]
    return ua, outs[0], uc, ud


def split_in_proj(w_in):
    c0 = np.cumsum((0,) + MIX_COLS)
    grp = lambda m: jnp.pad(w_in[:, c0[m]:c0[m + 1]], ((0, 0), (0, MIX_W[m] - MIX_COLS[m])))
    return jnp.concatenate([grp(0), grp(2), grp(3)], axis=1).astype(BF16), w_in[:, c0[1]:c0[2]].T.astype(BF16)


def _outproj_kernel(ya_ref, yb_ref, yc_ref, yd_ref, x_ref, gate_ref, g_ref, w_ref, o_ref):
    p = None
    for m, y_ref in enumerate((ya_ref, yb_ref, yc_ref, yd_ref)):
        t = _dot(y_ref[...].astype(BF16), w_ref[m * D_GROUP:(m + 1) * D_GROUP, :])
        p = t if p is None else p + t
    ms = jnp.mean(p * p, axis=-1, keepdims=True)
    o_ref[...] = x_ref[...] + gate_ref[0] * (p * lax.rsqrt(ms + EPS) * g_ref[...])


def proj_norm_residual(ys, xall, gate, g, w_bf16, ctx_len, tm=256):
    nb, ltot, d = xall.shape
    per_b = ltot // tm
    mod_map = _mod_map(per_b, ctx_len // tm, nb)
    row = lambda w: pl.BlockSpec((tm, w), lambda i: (i, 0))
    out = pl.pallas_call(
        _outproj_kernel,
        grid=(nb * per_b,),
        in_specs=[row(D_GROUP)] * N_MIXERS + [row(d), pl.BlockSpec((1, 1, d), mod_map), _full(1, d), _full(D_MIX, d)],
        out_specs=row(d),
        out_shape=jax.ShapeDtypeStruct((nb * ltot, d), F32),
        compiler_params=_params("arbitrary"),
        name="proj_norm_residual",
    )(*[y.reshape(nb * ltot, D_GROUP) for y in ys], xall.reshape(nb * ltot, d), gate.reshape(nb + 1, 1, d),
      g.reshape(1, d), w_bf16)
    return out.reshape(nb, ltot, d)


def _router_kernel(x_ref, sh_ref, sc_ref, g_ref, r_ref, rt_ref, hx_ref, afft_ref):
    h = _norm_mod(x_ref[...], g_ref[...], sh_ref[0], sc_ref[0])
    hx_ref[:, 0:D_MODEL] = h.astype(BF16)
    logits = _dot(h, r_ref[...], prec=HIGHEST)
    e = jnp.exp(logits - jnp.max(logits, axis=-1, keepdims=True))
    aff = e / jnp.sum(e, axis=-1, keepdims=True)
    a1 = aff.astype(BF16)
    r1 = aff - a1.astype(F32)
    a2 = r1.astype(BF16)
    a3 = (r1 - a2.astype(F32)).astype(BF16)
    pad = jnp.zeros((aff.shape[0], LANES - 3 * N_EXPERTS), BF16)
    hx_ref[:, D_MODEL:HX_W] = jnp.concatenate([a1, a2, a3, pad], axis=1)
    lt = _dot(rt_ref[...], h, NT, prec=HIGHEST)
    et = jnp.exp(lt - jnp.max(lt, axis=0, keepdims=True))
    afft_ref[0] = et / jnp.sum(et, axis=0, keepdims=True)


def norm_mod_router(xall, g, shift, scale, router, ctx_len, tm=256):
    nb, ltot, d = xall.shape
    e = router.shape[1]
    per_b = ltot // tm
    mod_map = _mod_map(per_b, ctx_len // tm, nb)
    hx, afft = pl.pallas_call(
        _router_kernel,
        grid=(nb * per_b,),
        in_specs=[pl.BlockSpec((tm, d), lambda i: (i, 0)),
                  pl.BlockSpec((1, 1, d), mod_map),
                  pl.BlockSpec((1, 1, d), mod_map),
                  _full(1, d), _full(d, e), _full(e, d)],
        out_specs=[pl.BlockSpec((tm, HX_W), lambda i: (i, 0)),
                   pl.BlockSpec((1, e, tm), lambda i: (i // per_b, 0, i % per_b))],
        out_shape=[jax.ShapeDtypeStruct((nb * ltot, HX_W), BF16),
                   jax.ShapeDtypeStruct((nb, e, ltot), F32)],
        compiler_params=_params("arbitrary"),
        name="norm_mod_router",
    )(xall.reshape(nb * ltot, d), shift.reshape(nb + 1, 1, d), scale.reshape(nb + 1, 1, d), g.reshape(1, d),
      router, router.T)
    return hx.reshape(nb, ltot, HX_W), afft


def _cap_pad(t_):
    cap = EC_CAPACITY * t_ // N_EXPERTS
    if cap < FFN_TM:
        return cap, cap + MOE_WIN, cap + MOE_WIN
    return cap, FFN_TM, -(-(cap + MOE_WIN) // FFN_TM) * FFN_TM


def _lane_prefix(x, t_):
    upper = jnp.where(_iota((LANES, LANES), 0) <= _iota((LANES, LANES), 1), 1.0, 0.0).astype(BF16)
    carry = jnp.zeros((x.shape[0], 1), F32)
    out = []
    for j in range(t_ // LANES):
        p = _dot(x[:, j * LANES:(j + 1) * LANES].astype(BF16), upper) + carry
        out.append(p)
        carry = p[:, LANES - 1:LANES]
    return jnp.concatenate(out, axis=1)


def _select_kernel(afft_ref, pos_ref, off_ref, *, cap, t_):
    aff = afft_ref[0]
    thr = jnp.zeros((aff.shape[0], 1), jnp.int32)
    for bit in range(30, -1, -1):
        cand = thr | (1 << bit)
        cnt = jnp.sum(jnp.where(aff >= pltpu.bitcast(cand, F32), 1, 0), axis=1, keepdims=True)
        thr = jnp.where(cnt >= cap, cand, thr)
    thr_f = pltpu.bitcast(thr, F32)
    gt = aff > thr_f
    eq = aff == thr_f
    need = (cap - jnp.sum(jnp.where(gt, 1, 0), axis=1, keepdims=True)).astype(F32)
    eq_rank = _lane_prefix(jnp.where(eq, 1.0, 0.0), t_)
    sel = gt | (eq & (eq_rank <= need))
    sel_f = jnp.where(sel, 1.0, 0.0)
    slot = _lane_prefix(sel_f, t_) - sel_f
    sel = sel & (slot < cap)
    sel_f = jnp.where(sel, 1.0, 0.0)
    pos_ref[0] = jnp.where(sel, slot, -1.0).astype(jnp.int32)
    nt = t_ // MOE_TOK
    before = jnp.where(_iota((t_, LANES), 0) < _iota((t_, LANES), 1) * MOE_TOK, 1.0, 0.0).astype(BF16)
    off_ref[0] = _dot(sel_f.astype(BF16), before)[:, 0:nt + 1].astype(jnp.int32)


def select_tokens(afft, t0, t_):
    nb, e, _ = afft.shape
    cap, _, _ = _cap_pad(t_)
    nt = t_ // MOE_TOK
    return pl.pallas_call(
        functools.partial(_select_kernel, cap=cap, t_=t_),
        grid=(nb,),
        in_specs=[pl.BlockSpec((1, e, t_), lambda b: (b, 0, t0 // t_))],
        out_specs=[pl.BlockSpec((1, e, t_), lambda b: (b, 0, 0)), pl.BlockSpec((1, e, nt + 1), lambda b: (b, 0, 0))],
        out_shape=[jax.ShapeDtypeStruct((nb, e, t_), jnp.int32), jax.ShapeDtypeStruct((nb, e, nt + 1), jnp.int32)],
        compiler_params=_params("arbitrary"),
        name="moe_select",
    )(afft)


def _window(off_ref, b, e, j):
    off = off_ref[b, e, j]
    start = pl.multiple_of((off // SUBLANES_BF16) * SUBLANES_BF16, SUBLANES_BF16)
    nwin = (off_ref[b, e, j + 1] - start + MOE_WIN - 1) // MOE_WIN
    return start, nwin


def _gather_kernel(off_ref, pos_ref, hx_ref, xs_ref, *, nt):
    b, eg = pl.program_id(0), pl.program_id(1)
    xs_ref[...] = jnp.zeros_like(xs_ref)
    r = _iota((MOE_WIN, MOE_TOK), 0)

    def put(i, s0, got):
        head = pl.ds(s0, SUBLANES_BF16)
        xs_ref[0, i, head, :] = xs_ref[0, i, head, :] + got[0:SUBLANES_BF16]
        xs_ref[0, i, pl.ds(s0 + SUBLANES_BF16, MOE_WIN - SUBLANES_BF16), :] = got[SUBLANES_BF16:MOE_WIN]

    def tile(j, carry):
        rows = hx_ref[0, pl.ds(pl.multiple_of(j * MOE_TOK, MOE_TOK), MOE_TOK), :]
        wins = [_window(off_ref, b, eg * GATHER_E + i, j) for i in range(GATHER_E)]
        pos = [pos_ref[0, i, pl.ds(j, 1), :] for i in range(GATHER_E)]
        onehot = jnp.concatenate([jnp.where(pos[i] - wins[i][0] == r, 1.0, 0.0) for i in range(GATHER_E)], axis=0)
        got = _dot(onehot.astype(BF16), rows).astype(BF16)
        for i in range(GATHER_E):
            put(i, wins[i][0], got[i * MOE_WIN:(i + 1) * MOE_WIN])

            def more(p, c2, i=i):
                s0 = pl.multiple_of(wins[i][0] + p * MOE_WIN, SUBLANES_BF16)
                put(i, s0, _dot(jnp.where(pos[i] - s0 == r, 1.0, 0.0).astype(BF16), rows).astype(BF16))
                return c2

            lax.fori_loop(1, wins[i][1], more, 0)
        return carry

    lax.fori_loop(0, nt, tile, 0)


def gather_tokens(hx, pos, off, t0):
    nb, _, w = hx.shape
    e, nt = off.shape[1], off.shape[2] - 1
    t_ = nt * MOE_TOK
    _, _, cpad = _cap_pad(t_)
    return pl.pallas_call(
        functools.partial(_gather_kernel, nt=nt),
        grid_spec=pltpu.PrefetchScalarGridSpec(
            num_scalar_prefetch=1,
            grid=(nb, e // GATHER_E),
            in_specs=[pl.BlockSpec((1, GATHER_E, nt, MOE_TOK), lambda b, g, off_: (b, g, 0, 0)),
                      pl.BlockSpec((1, t_, w), lambda b, g, off_: (b, t0 // t_, 0), pipeline_mode=pl.Buffered(1))],
            out_specs=pl.BlockSpec((1, GATHER_E, cpad, w), lambda b, g, off_: (b, g, 0, 0))),
        out_shape=jax.ShapeDtypeStruct((nb, e, cpad, w), BF16),
        compiler_params=_params("arbitrary", "arbitrary"),
        name="moe_gather",
    )(off, pos.reshape(nb, e, nt, MOE_TOK), hx)


def _ffn_kernel(xs_ref, wg_ref, wu_ref, wd_ref, y_ref, w_ref, *, n_real):
    e, b, j = pl.program_id(0), pl.program_id(1), pl.program_id(2)

    @pl.when((b == 0) & (j == 0))
    def _():
        w_ref[0] = wg_ref[0, 0].astype(BF16)
        w_ref[1] = wu_ref[0, 0].astype(BF16)
        w_ref[2] = wd_ref[0, 0].astype(BF16)

    @pl.when(j < n_real)
    def _():
        x = xs_ref[0, 0, :, 0:D_MODEL]
        a = _dot(x, w_ref[0])
        u = _dot(x, w_ref[1])
        y = _dot((_silu(a) * u).astype(BF16), w_ref[2])
        aff = xs_ref[0, 0, :, D_MODEL:HX_W].astype(F32)
        lane = _iota(aff.shape, 1)
        mine = (lane % N_EXPERTS == e) & (lane < 3 * N_EXPERTS)
        y_ref[0, 0] = (y * jnp.sum(jnp.where(mine, aff, 0.0), axis=1, keepdims=True)).astype(BF16)

    @pl.when(j >= n_real)
    def _():
        y_ref[...] = jnp.zeros_like(y_ref)


def expert_ffn(xs, wg, wu, wd, layer, cap, tm):
    nb, e, cpad, w = xs.shape
    d, f = wg.shape[2], wg.shape[3]
    assert d == f, "the bf16 weight scratch holds the three matrices as one (3, D, F) array"
    n_real = -(-cap // tm)
    return pl.pallas_call(
        functools.partial(_ffn_kernel, n_real=n_real),
        grid=(e, nb, cpad // tm),
        in_specs=[pl.BlockSpec((1, 1, tm, w), lambda i, b, j: (b, i, jnp.minimum(j, n_real - 1), 0)),
                  pl.BlockSpec((1, 1, d, f), lambda i, b, j: (layer, i, 0, 0)),
                  pl.BlockSpec((1, 1, d, f), lambda i, b, j: (layer, i, 0, 0)),
                  pl.BlockSpec((1, 1, f, d), lambda i, b, j: (layer, i, 0, 0))],
        out_specs=pl.BlockSpec((1, 1, tm, d), lambda i, b, j: (b, i, j, 0)),
        out_shape=jax.ShapeDtypeStruct((nb, e, cpad, d), BF16),
        scratch_shapes=[pltpu.VMEM((3, d, f), BF16)],
        compiler_params=_params("arbitrary", "arbitrary", "arbitrary"),
        name="expert_ffn",
    )(xs, wg, wu, wd)


def _combine_kernel(off_ref, posT_ref, x_ref, gate_ref, g_ref, y_hbm, o_ref, ybuf, acc_ref, sem):
    b, j = pl.program_id(0), pl.program_id(1)
    ne = N_EXPERTS
    lane = _iota((MOE_TOK, MOE_WIN), 1)
    starts, nwins = zip(*[_window(off_ref, b, e, j) for e in range(ne)])
    nmax = functools.reduce(jnp.maximum, nwins)

    def copies(p):
        out = []
        for e in range(ne):
            s0 = pl.multiple_of(jnp.where(p < nwins[e], starts[e] + p * MOE_WIN, 0), SUBLANES_BF16)
            out.append((s0, pltpu.make_async_copy(y_hbm.at[b, e, pl.ds(s0, MOE_WIN), :],
                                                  ybuf.at[pl.ds(e * MOE_WIN, MOE_WIN), :], sem.at[e])))
        return out

    def window_sum(p):
        cps = copies(p)
        for _, cp in cps:
            cp.start()
        cols = []
        for e in range(ne):
            pos_e = posT_ref[0, :, e:e + 1]
            hit = (pos_e - cps[e][0] == lane) & (p < nwins[e])
            cols.append(jnp.where(hit, 1.0, 0.0).astype(BF16))
        onehot = jnp.concatenate(cols, axis=1)
        for _, cp in cps:
            cp.wait()
        return _dot(onehot, ybuf[...])

    acc_ref[...] = window_sum(0)

    def more(p, c):
        acc_ref[...] += window_sum(p)
        return c

    lax.fori_loop(1, nmax, more, 0)
    m = acc_ref[...]
    ms = jnp.mean(m * m, axis=-1, keepdims=True)
    o_ref[0] = x_ref[0] + gate_ref[0] * (m * lax.rsqrt(ms + EPS) * g_ref[...])


def combine_residual(y, pos, off, xall, t0, gate, g):
    nb, e, cpad, d = y.shape
    nt = off.shape[2] - 1
    t_ = nt * MOE_TOK
    pos_t = pos.transpose(0, 2, 1)
    nmod = gate.shape[0]
    gate_map = (lambda b, j, off_: (b, 0, 0)) if nmod > 1 else (lambda b, j, off_: (0, 0, 0))
    return pl.pallas_call(
        _combine_kernel,
        grid_spec=pltpu.PrefetchScalarGridSpec(
            num_scalar_prefetch=1,
            grid=(nb, nt),
            in_specs=[pl.BlockSpec((1, MOE_TOK, e), lambda b, j, off_: (b, j, 0)),
                      pl.BlockSpec((1, MOE_TOK, d), lambda b, j, off_: (b, t0 // MOE_TOK + j, 0)),
                      pl.BlockSpec((1, 1, d), gate_map),
                      pl.BlockSpec((1, d), lambda b, j, off_: (0, 0)),
                      pl.BlockSpec(memory_space=pl.ANY)],
            out_specs=pl.BlockSpec((1, MOE_TOK, d), lambda b, j, off_: (b, j, 0)),
            scratch_shapes=[pltpu.VMEM((e * MOE_WIN, d), BF16), pltpu.VMEM((MOE_TOK, d), F32),
                            pltpu.SemaphoreType.DMA((e,))]),
        out_shape=jax.ShapeDtypeStruct((nb, t_, d), F32),
        compiler_params=_params("arbitrary", "arbitrary"),
        name="moe_combine",
    )(off, pos_t, xall, gate.reshape(nmod, 1, d), g.reshape(1, d), y)


def moe_residual(hx, afft, xall, t0, t_, gate, g, wg, wu, wd, layer):
    cap, tm, _ = _cap_pad(t_)
    pos, off = select_tokens(afft, t0, t_)
    xs = gather_tokens(hx, pos, off, t0)
    y = expert_ffn(xs, wg, wu, wd, layer, cap, tm)
    return combine_residual(y, pos, off, xall, t0, gate, g)


def _tri_tile(tile, reverse):
    r, c = _iota((tile, tile), 0), _iota((tile, tile), 1)
    return jnp.where((r // CHUNK == c // CHUNK) & ((r <= c) if reverse else (r >= c)), 1.0, 0.0).astype(F32)


def _block_mask(shape, rblk, cblk):
    return (_iota(shape, 0) // rblk) == (_iota(shape, 1) // cblk)


def _group_sum(x, ones_blk):
    hi = x.astype(BF16)
    lo = (x - hi.astype(F32)).astype(BF16)
    return _dot(hi, ones_blk) + _dot(lo, ones_blk)


def _scan_tile(j, reverse, nt):
    return jnp.where(j == 0, nt - 1, (nt - 1 - j) if reverse else (j - 1))


def _tile_order(reverse, nt):
    return lambda j: (0, _scan_tile(j, reverse, nt), 0)


def _tile_index(reverse, nt):
    return _scan_tile(pl.program_id(0), reverse, nt)


def _halo_specs(nb, tile, w, reverse, nt):
    per = tile // HALO
    main = _tile_order(reverse, nt)
    last_blk = nt * per - 1

    def prev(j):
        return (0, jnp.maximum(main(j)[1] * per - 1, 0), 0)

    def nxt(j):
        return (0, jnp.minimum((main(j)[1] + 1) * per, last_blk), 0)

    return [pl.BlockSpec((nb, tile, w), main), pl.BlockSpec((nb, HALO, w), prev), pl.BlockSpec((nb, HALO, w), nxt)]


def _short_conv_tile(x, prev_row, next_row, w_ref, bias):
    t = x.shape[0]
    r = _iota(x.shape, 0)
    xm1 = jnp.where(r == 0, prev_row, pltpu.roll(x, 1, 0))
    xp1 = jnp.where(r == t - 1, next_row, pltpu.roll(x, t - 1, 0))
    y = xm1 * w_ref[0:1, :] + x * w_ref[1:2, :] + xp1 * w_ref[2:3, :]
    return y if bias is None else y + bias


def _conv_halo_ok(t, nt):
    return jnp.where((t >= 1) & (t <= nt - 2), 1.0, 0.0), jnp.where(t <= nt - 3, 1.0, 0.0)


def _scan_calls(kernel_fn, name, u_specs_fn, u_args, param_specs, params, scratch, nb, ltot, tile):
    nt = ltot // tile

    def call(reverse, extra):
        idx = _tile_order(reverse, nt)
        in_specs = u_specs_fn(reverse, nt) + list(param_specs)
        if reverse:
            in_specs.append(pl.BlockSpec((nb, tile, D_GROUP), idx))
        return pl.pallas_call(
            functools.partial(kernel_fn, reverse=reverse, nb=nb, nchunks=tile // CHUNK, nt=nt),
            grid=(nt,),
            in_specs=in_specs,
            out_specs=pl.BlockSpec((nb, tile, D_GROUP), idx),
            out_shape=jax.ShapeDtypeStruct((nb, ltot, D_GROUP), F32),
            scratch_shapes=scratch,
            compiler_params=_params("arbitrary"),
            name=name + ("_bwd" if reverse else "_fwd"),
        )(*u_args, *params, *extra)

    return call(True, (call(False, ()),))


def _gla_kernel(u_ref, w2_ref, b_ref, nw_ref, *rest, reverse, nb, nchunks, nt):
    if reverse:
        of_ref, o_ref, s_ref, cum_ref = rest
    else:
        o_ref, s_ref, cum_ref = rest
    nq = GLA_HEADS * GLA_DK
    tile = nchunks * CHUNK

    @pl.when(pl.program_id(0) == 0)
    def _():
        s_ref[...] = jnp.zeros_like(s_ref)

    d = 1 if reverse else 0
    w2 = w2_ref[d]
    bias = b_ref[d]
    r64, c256 = _iota((CHUNK, D_GROUP), 0), _iota((CHUNK, D_GROUP), 1) % CHUNK
    causal = (r64 <= c256) if reverse else (r64 >= c256)
    mask_k = _block_mask((D_GROUP, nq), CHUNK, GLA_DK)
    mask_v = _block_mask((D_GROUP, D_GROUP), CHUNK, GLA_DV)
    ones_blk = jnp.where(mask_v, 1.0, 0.0).astype(BF16)
    tri_tile = _tri_tile(tile, reverse)

    lr0 = 2 * nq + 2 * D_GROUP + d * GLA_RANK
    for b in range(nb):
        z = _dot(u_ref[b, :, lr0:lr0 + GLA_RANK], w2, prec=HIGHEST) + bias
        cum_ref[b] = _dot(tri_tile, jax.nn.log_sigmoid(z) / GLA_TAU, prec=HIGHEST)

    last = 0 if reverse else CHUNK - 1
    order = range(nchunks - 1, -1, -1) if reverse else range(nchunks)
    for c in order:
        rows = pl.ds(c * CHUNK, CHUNK)
        for b in range(nb):
            q = u_ref[b, rows, 0:nq] * (GLA_DK ** -0.5)
            k = u_ref[b, rows, nq:2 * nq]
            v = u_ref[b, rows, 2 * nq:2 * nq + D_GROUP]
            cum = cum_ref[b, rows, :]
            cum_last = cum[last:last + 1, :]
            qg = (q * jnp.exp(cum)).astype(BF16)
            kg = k * jnp.exp(-cum)
            kend = (k * jnp.exp(cum_last - cum)).astype(BF16)
            dec = jnp.exp(cum_last)
            k_st = jnp.where(mask_k, jnp.concatenate([kg] * GLA_HEADS, axis=0), 0.0).astype(BF16)
            attn = jnp.where(causal, _dot(qg, k_st, NT), 0.0).astype(BF16)
            vb = v.astype(BF16)
            v_bd = jnp.where(mask_v, jnp.concatenate([vb] * GLA_HEADS, axis=0), jnp.zeros((), BF16))
            s_t = s_ref[b]
            o = _dot(attn, v_bd) + _dot(qg, s_t.astype(BF16), NT)
            s_ref[b] = s_t * dec + jnp.where(mask_k, _dot(vb, kend, TN), 0.0)
            o_ref[b, rows, :] = o
    if reverse:
        for b in range(nb):
            o = o_ref[b] + of_ref[b]
            ms = _group_sum(o * o, ones_blk) * (1.0 / GLA_DV)
            og = u_ref[b, :, 2 * nq + D_GROUP:2 * nq + 2 * D_GROUP]
            o_ref[b] = o * lax.rsqrt(ms + EPS) * nw_ref[...] * _silu(og)


def gla_mixer(u, w2, b, norm_w, tile):
    nb, ltot, w = u.shape
    nq = GLA_HEADS * GLA_DK
    u_specs = lambda reverse, nt: [pl.BlockSpec((nb, tile, w), _tile_order(reverse, nt))]
    return _scan_calls(_gla_kernel, "gla", u_specs, (u,),
                       [_full(2, GLA_RANK, nq), _full(2, 1, nq), _full(1, D_GROUP)],
                       (w2, b.reshape(2, 1, nq), norm_w.reshape(1, D_GROUP)),
                       [pltpu.VMEM((nb, D_GROUP, nq), F32), pltpu.VMEM((nb, tile, nq), F32)], nb, ltot, tile)


def _ssd_kernel(u_ref, up_ref, un_ref, cw_ref, cb_ref, alog_ref, dtb_ref, dsk_ref, nw_ref, *rest,
                reverse, nb, nchunks, nt):
    if reverse:
        of_ref, o_ref, s_ref, xbc_ref, g256_ref, g512_ref = rest
    else:
        o_ref, s_ref, xbc_ref, g256_ref, g512_ref = rest
    t = _tile_index(reverse, nt)
    tile = nchunks * CHUNK

    @pl.when(pl.program_id(0) == 0)
    def _():
        s_ref[...] = jnp.zeros_like(s_ref)

    d = 1 if reverse else 0
    a = -jnp.exp(alog_ref[d:d + 1, :])
    dtb = dtb_ref[d:d + 1, :]
    r64, c256 = _iota((CHUNK, D_GROUP), 0), _iota((CHUNK, D_GROUP), 1) % CHUNK
    causal = (r64 <= c256) if reverse else (r64 >= c256)
    tri_t = jnp.where((r64 >= c256) if reverse else (r64 <= c256), 1.0, 0.0).astype(F32)
    e256 = jnp.where(_iota((MB_HEADS, D_GROUP), 0) == _iota((MB_HEADS, D_GROUP), 1) // CHUNK, 1.0, 0.0).astype(F32)
    hn = MB_HEADS * MB_STATE
    e512 = jnp.where(_iota((MB_HEADS, hn), 0) == _iota((MB_HEADS, hn), 1) // MB_STATE, 1.0, 0.0).astype(F32)
    gw = MB_GROUPS * MB_STATE
    rep = MB_HEADS // MB_GROUPS
    mask_b = (_iota((D_GROUP, gw), 0) // (CHUNK * rep)) == (_iota((D_GROUP, gw), 1) // MB_STATE)
    mask_v = _block_mask((D_GROUP, D_GROUP), CHUNK, MB_HEADDIM)
    mask_s = _block_mask((hn, D_GROUP), MB_STATE, MB_HEADDIM)
    ones_blk = jnp.where(mask_v, 1.0, 0.0).astype(BF16)
    tri_tile = _tri_tile(tile, reverse)

    prev_ok, next_ok = _conv_halo_ok(t, nt)
    c0, c1 = D_GROUP, D_GROUP + MB_CONV_CH
    dt0 = D_GROUP + MB_CONV_CH + d * MB_HEADS
    for b in range(nb):
        x = u_ref[b, :, c0:c1]
        pr = up_ref[b, HALO - 1:HALO, c0:c1] * prev_ok
        nx = un_ref[b, 0:1, c0:c1] * next_ok
        xbc_ref[b] = _silu(_short_conv_tile(x, pr, nx, cw_ref, cb_ref[...]))
        dt = jax.nn.softplus(u_ref[b, :, dt0:dt0 + MB_HEADS] + dtb)
        da = dt * a
        da_b = _dot(da, e256, prec=HIGHEST)
        g256_ref[b, 0] = _dot(dt, e256, prec=HIGHEST)
        g256_ref[b, 1] = da_b
        g256_ref[b, 2] = _dot(tri_tile, da_b, prec=HIGHEST)
        g512_ref[b] = _dot(tri_tile, _dot(da, e512, prec=HIGHEST), prec=HIGHEST)

    last = 0 if reverse else CHUNK - 1
    order = range(nchunks - 1, -1, -1) if reverse else range(nchunks)
    for c in order:
        rows = pl.ds(c * CHUNK, CHUNK)
        for b in range(nb):
            xs = xbc_ref[b, rows, 0:D_GROUP]
            bm = xbc_ref[b, rows, D_GROUP:D_GROUP + gw]
            cm = xbc_ref[b, rows, D_GROUP + gw:D_GROUP + 2 * gw]
            da_b = g256_ref[b, 1, rows, :]
            cum_b = g256_ref[b, 2, rows, :]
            cum512 = g512_ref[b, rows, :]
            cum_row = jnp.sum(da_b * tri_t, axis=0, keepdims=True)
            seg = jnp.exp(jnp.where(causal, cum_b - cum_row, -jnp.inf))
            cum_last512 = cum512[last:last + 1, :]
            xdt = xs * g256_ref[b, 0, rows, :]
            xdt_b = xdt.astype(BF16)
            b_st = jnp.where(mask_b, jnp.concatenate([bm] * MB_HEADS, axis=0), 0.0).astype(BF16)
            lmat = (_dot(cm.astype(BF16), b_st, NT) * seg).astype(BF16)
            xdt_bd = jnp.where(mask_v, jnp.concatenate([xdt_b] * MB_HEADS, axis=0), jnp.zeros((), BF16))
            c_cat = jnp.concatenate([cm[:, (h // rep) * MB_STATE:(h // rep + 1) * MB_STATE] for h in range(MB_HEADS)], 1)
            b_cat = jnp.concatenate([bm[:, (h // rep) * MB_STATE:(h // rep + 1) * MB_STATE] for h in range(MB_HEADS)], 1)
            c_cat = (c_cat * jnp.exp(cum512)).astype(BF16)
            b_cat = (b_cat * jnp.exp(cum_last512 - cum512)).astype(BF16)
            s = s_ref[b]
            y = _dot(lmat, xdt_bd) + _dot(c_cat, s.astype(BF16))
            dec = jnp.exp(cum_b[last:last + 1, :])
            s_ref[b] = s * dec + jnp.where(mask_s, _dot(b_cat, xdt_b, TN), 0.0)
            o_ref[b, rows, :] = y
    if reverse:
        for b in range(nb):
            y = o_ref[b] + of_ref[b]
            yy = (y + xbc_ref[b, :, 0:D_GROUP] * dsk_ref[...]) * _silu(u_ref[b, :, 0:D_GROUP])
            ms = _group_sum(yy * yy, ones_blk) * (1.0 / MB_HEADDIM)
            o_ref[b] = yy * lax.rsqrt(ms + EPS) * nw_ref[...]


def mamba_mixer(u, conv_w, conv_b, a_log, dt_bias, d_skip, norm_w, tile):
    nb, ltot, w = u.shape
    d_cat = jnp.repeat(d_skip.astype(F32), MB_HEADDIM).reshape(1, D_GROUP)
    u_specs = lambda reverse, nt: _halo_specs(nb, tile, w, reverse, nt)
    return _scan_calls(_ssd_kernel, "ssd", u_specs, (u, u, u),
                       [_full(SHORT_CONV, MB_CONV_CH), _full(1, MB_CONV_CH), _full(2, MB_HEADS), _full(2, MB_HEADS),
                        _full(1, D_GROUP), _full(1, D_GROUP)],
                       (conv_w, conv_b.reshape(1, -1), a_log, dt_bias, d_cat, norm_w.reshape(1, D_GROUP)),
                       [pltpu.VMEM((nb, MB_HEADS * MB_STATE, D_GROUP), F32), pltpu.VMEM((nb, tile, MB_CONV_CH), F32),
                        pltpu.VMEM((nb, 3, tile, D_GROUP), F32), pltpu.VMEM((nb, tile, MB_HEADS * MB_STATE), F32)],
                       nb, ltot, tile)


def _to_bd(x_cat, mask):
    return jnp.where(mask, jnp.concatenate([x_cat] * DN_HEADS, axis=0), jnp.zeros((), x_cat.dtype))


def _dn_kernel(u_ref, up_ref, un_ref, cw_ref, alog_ref, dtb_ref, nw_ref, *rest, reverse, nb, nchunks, nt):
    if reverse:
        of_ref, o_ref, s_ref, qkv_ref, gate_ref = rest
    else:
        o_ref, s_ref, qkv_ref, gate_ref = rest
    t = _tile_index(reverse, nt)
    tile = nchunks * CHUNK

    @pl.when(pl.program_id(0) == 0)
    def _():
        s_ref[...] = jnp.zeros_like(s_ref)

    d = 1 if reverse else 0
    neg_a = -jnp.exp(alog_ref[d:d + 1, :])
    dtb = dtb_ref[d:d + 1, :]
    r64, c256 = _iota((CHUNK, D_GROUP), 0), _iota((CHUNK, D_GROUP), 1) % CHUNK
    incl = (r64 <= c256) if reverse else (r64 >= c256)
    strict = (r64 < c256) if reverse else (r64 > c256)
    tri_t = jnp.where((r64 >= c256) if reverse else (r64 <= c256), 1.0, 0.0).astype(F32)
    eye_cat = jnp.where(r64 == c256, 1.0, 0.0).astype(F32)
    same_blk = [(r64 // s) == (c256 // s) for s in (8, 16, 32, 64)]
    e256 = jnp.where(_iota((DN_HEADS, D_GROUP), 0) == _iota((DN_HEADS, D_GROUP), 1) // CHUNK, 1.0, 0.0).astype(F32)
    mask = _block_mask((D_GROUP, D_GROUP), CHUNK, DN_DK)
    ones_blk = jnp.where(mask, 1.0, 0.0).astype(BF16)
    tri_tile = _tri_tile(tile, reverse)

    prev_ok, next_ok = _conv_halo_ok(t, nt)
    b0 = 4 * D_GROUP + d * DN_HEADS
    a0 = 4 * D_GROUP + N_DIR * DN_HEADS + d * DN_HEADS
    for b in range(nb):
        x = u_ref[b, :, 0:DN_QKV]
        pr = up_ref[b, HALO - 1:HALO, 0:DN_QKV] * prev_ok
        nx = un_ref[b, 0:1, 0:DN_QKV] * next_ok
        qkv = _silu(_short_conv_tile(x, pr, nx, cw_ref, None))
        q, k = qkv[:, 0:D_GROUP], qkv[:, D_GROUP:2 * D_GROUP]
        qkv_ref[b, :, 0:D_GROUP] = q * lax.rsqrt(_group_sum(q * q, ones_blk) + EPS) * (DN_DK ** -0.5)
        qkv_ref[b, :, D_GROUP:2 * D_GROUP] = k * lax.rsqrt(_group_sum(k * k, ones_blk) + EPS)
        qkv_ref[b, :, 2 * D_GROUP:3 * D_GROUP] = qkv[:, 2 * D_GROUP:3 * D_GROUP]
        beta = jax.nn.sigmoid(u_ref[b, :, b0:b0 + DN_HEADS])
        g = neg_a * jax.nn.softplus(u_ref[b, :, a0:a0 + DN_HEADS] + dtb)
        g_b = _dot(g, e256, prec=HIGHEST)
        gate_ref[b, 0] = _dot(beta, e256, prec=HIGHEST)
        gate_ref[b, 1] = g_b
        gate_ref[b, 2] = _dot(tri_tile, g_b, prec=HIGHEST)

    last = 0 if reverse else CHUNK - 1
    order = range(nchunks - 1, -1, -1) if reverse else range(nchunks)
    for c in order:
        rows = pl.ds(c * CHUNK, CHUNK)
        for b in range(nb):
            qn = qkv_ref[b, rows, 0:D_GROUP]
            kn = qkv_ref[b, rows, D_GROUP:2 * D_GROUP]
            v = qkv_ref[b, rows, 2 * D_GROUP:3 * D_GROUP]
            beta_b = gate_ref[b, 0, rows, :]
            g_b = gate_ref[b, 1, rows, :]
            cum_b = gate_ref[b, 2, rows, :]
            cum_row = jnp.sum(g_b * tri_t, axis=0, keepdims=True)
            decay = jnp.exp(jnp.where(incl, cum_b - cum_row, -jnp.inf))
            cum_last = cum_b[last:last + 1, :]
            e_cum = jnp.exp(cum_b)
            k_beta = kn * beta_b
            k_st = _to_bd(kn.astype(BF16), mask)
            m_cat = jnp.where(strict, _dot(k_beta.astype(BF16), k_st, NT) * decay, 0.0)
            attn = (_dot(qn.astype(BF16), k_st, NT) * decay).astype(BF16)
            m8 = jnp.where(same_blk[0], m_cat, 0.0).astype(BF16)
            t_cat = eye_cat - m8
            pw = _dot(m8, _to_bd(m8, mask))
            both = _dot(jnp.concatenate([t_cat, pw], axis=0).astype(BF16), _to_bd(pw.astype(BF16), mask))
            t_cat = t_cat + both[0:CHUNK]
            t_cat = t_cat + _dot(t_cat.astype(BF16), _to_bd(both[CHUNK:2 * CHUNK].astype(BF16), mask))
            for lvl in range(1, len(same_blk)):
                m_off = jnp.where(same_blk[lvl] & jnp.logical_not(same_blk[lvl - 1]), m_cat, 0.0).astype(BF16)
                tm_off = _dot(t_cat.astype(BF16), _to_bd(m_off, mask))
                t_cat = t_cat - _dot(tm_off.astype(BF16), _to_bd(t_cat.astype(BF16), mask))
            rhs = jnp.concatenate([_to_bd((v * beta_b).astype(BF16), mask),
                                   _to_bd((k_beta * e_cum).astype(BF16), mask)], axis=1)
            sol = _dot(t_cat.astype(BF16), rhs)
            u_val, w_key = sol[:, 0:D_GROUP], sol[:, D_GROUP:2 * D_GROUP]
            s = s_ref[b]
            s_b = s.astype(BF16)
            v_new = u_val - _dot(w_key.astype(BF16), s_b)
            v_new_b = v_new.astype(BF16)
            o = _dot((qn * e_cum).astype(BF16), s_b) + _dot(attn, _to_bd(v_new_b, mask))
            k_end = (kn * jnp.exp(cum_last - cum_b)).astype(BF16)
            s_ref[b] = s * jnp.exp(cum_last) + jnp.where(mask, _dot(k_end, v_new_b, TN), 0.0)
            o_ref[b, rows, :] = o
    if reverse:
        for b in range(nb):
            o = o_ref[b] + of_ref[b]
            ms = _group_sum(o * o, ones_blk) * (1.0 / DN_DV)
            o_ref[b] = o * lax.rsqrt(ms + EPS) * nw_ref[...] * _silu(u_ref[b, :, 3 * D_GROUP:4 * D_GROUP])


def deltanet_mixer(u, conv_w, a_log, dt_bias, norm_w, tile):
    nb, ltot, w = u.shape
    u_specs = lambda reverse, nt: _halo_specs(nb, tile, w, reverse, nt)
    return _scan_calls(_dn_kernel, "dn", u_specs, (u, u, u),
                       [_full(SHORT_CONV, DN_QKV), _full(2, DN_HEADS), _full(2, DN_HEADS), _full(1, D_GROUP)],
                       (conv_w, a_log, dt_bias, norm_w.reshape(1, D_GROUP)),
                       [pltpu.VMEM((nb, D_GROUP, D_GROUP), F32), pltpu.VMEM((nb, tile, DN_QKV), F32),
                        pltpu.VMEM((nb, 3, tile, D_GROUP), F32)],
                       nb, ltot, tile)


HY_CB = 16
HY_FROWS = 8


def _dft_constants(nr):
    k = np.arange(nr)[:, None] * np.arange(nr)[None, :]
    a, b = np.cos(2 * np.pi * k / nr), -np.sin(2 * np.pi * k / nr)
    kl = np.arange(LANES)[:, None] * np.arange(LANES)[None, :]
    cr, ci = np.cos(2 * np.pi * kl / LANES), -np.sin(2 * np.pi * kl / LANES)
    kt = np.arange(nr)[:, None] * np.arange(LANES)[None, :]
    tr, ti = np.cos(2 * np.pi * kt / (nr * LANES)), -np.sin(2 * np.pi * kt / (nr * LANES))
    h = nr // 2
    return dict(
        ab_half=jnp.asarray(np.concatenate([a[:, :h], b[:, :h]], 0), F32),
        ab_full=jnp.asarray(np.concatenate([a, b], 0), F32),
        atbt=jnp.asarray(np.concatenate([a[:, :h].T, b[:, :h].T], 0), F32),
        cc=jnp.asarray(np.concatenate([cr, ci], 1), F32),
        tw=jnp.asarray(np.stack([tr, ti]), F32))


def _rows_fwd(ab, xs, nr):
    rhs = jnp.concatenate([jnp.concatenate([re, im], axis=1) for re, im in xs], axis=1).astype(BF16)
    out = _dot(ab, rhs)
    res = []
    for i in range(len(xs)):
        blk = out[:, 2 * LANES * i:2 * LANES * (i + 1)]
        res.append((blk[0:nr, 0:LANES] - blk[nr:2 * nr, LANES:2 * LANES],
                    blk[0:nr, LANES:2 * LANES] + blk[nr:2 * nr, 0:LANES]))
    return res


def _lanes_dft(cc, zs, nr, conj):
    lhs = jnp.concatenate([t for z in zs for t in z], axis=0).astype(BF16)
    out = _dot(lhs, cc)
    res = []
    for i in range(len(zs)):
        rr = out[2 * nr * i:2 * nr * i + nr, 0:LANES]
        ri = out[2 * nr * i:2 * nr * i + nr, LANES:2 * LANES]
        ir = out[2 * nr * i + nr:2 * nr * (i + 1), 0:LANES]
        ii = out[2 * nr * i + nr:2 * nr * (i + 1), LANES:2 * LANES]
        res.append((rr + ii, ir - ri) if conj else (rr - ii, ri + ir))
    return res


def _rows_inv(atbt, vs, nr):
    h = nr // 2
    rhs = jnp.concatenate([jnp.concatenate([re, im], axis=1) for re, im in vs], axis=1).astype(BF16)
    out = _dot(atbt, rhs)
    res = []
    for i in range(len(vs)):
        blk = out[:, 2 * LANES * i:2 * LANES * (i + 1)]
        res.append((blk[0:h, 0:LANES] + blk[h:nr, LANES:2 * LANES], blk[0:h, LANES:2 * LANES] - blk[h:nr, 0:LANES]))
    return res


def _cmul(x, y, conj=False):
    (xr, xi), (yr, yi) = x, y
    if conj:
        return xr * yr + xi * yi, xi * yr - xr * yi
    return xr * yr - xi * yi, xr * yi + xi * yr


def _conv_pairs(pairs, spec, consts, nr):
    ab, atbt, cc, tw = consts
    y = _rows_fwd(ab, pairs, nr)
    z = [_cmul(t, tw) for t in y]
    x = _lanes_dft(cc, z, nr, False)
    g = [_cmul(t, spec) for t in x]
    u = _lanes_dft(cc, g, nr, True)
    v = [_cmul(t, tw, conj=True) for t in u]
    scale = 1.0 / (nr * LANES)
    return [(re * scale, im * scale) for re, im in _rows_inv(atbt, v, nr)]


def _shift_conv(x, w0, w1, w2, bias):
    rows, lane = _iota(x.shape, 0), _iota(x.shape, 1)
    r1 = pltpu.roll(x, 1, 1)
    xm1 = jnp.where(lane == 0, jnp.where(rows == 0, 0.0, pltpu.roll(r1, 1, 0)), r1)
    l1 = pltpu.roll(x, LANES - 1, 1)
    xp1 = jnp.where(lane == LANES - 1, jnp.where(rows == x.shape[0] - 1, 0.0, pltpu.roll(l1, x.shape[0] - 1, 0)), l1)
    return xm1 * w0 + x * w1 + xp1 * w2 + bias


def _hy_conv_kernel(p_ref, v_ref, x1_ref, x2_ref, k_ref, abh_ref, abf_ref, atbt_ref, cc_ref, tw_ref, o_ref,
                    *, nseq, nr, ctx_rows):
    g, jc = pl.program_id(0), pl.program_id(1)
    h = nr // 2
    consts = (abh_ref[...].astype(BF16), atbt_ref[...].astype(BF16), cc_ref[...].astype(BF16), (tw_ref[0], tw_ref[1]))
    ab_full = abf_ref[...].astype(BF16)
    valid = _iota((h, LANES), 0) < jnp.where(g == 0, h, ctx_rows)
    zero = jnp.zeros((nr, LANES), F32)
    for i in range(HY_CB):
        c = jc * HY_CB + i
        parts = []
        for part, ref in enumerate((v_ref, x1_ref, x2_ref)):
            ch = part * D_GROUP + c
            w0, w1, w2, bias = p_ref[0, ch], p_ref[1, ch], p_ref[2, ch], p_ref[3, ch]
            parts.append([jnp.where(valid, _shift_conv(jnp.where(valid, ref[s, i], 0.0), w0, w1, w2, bias), 0.0)
                          for s in range(nseq)])
        v, x1, x2 = parts
        kf = _rows_fwd(ab_full, [(k_ref[0, o, i], zero) for o in range(HY_ORDER)], nr)
        kf = _lanes_dft(consts[2], [_cmul(t, consts[3]) for t in kf], nr, False)
        cur = v
        for o in range(HY_ORDER):
            d = p_ref[4 + o, c]
            conv = _conv_pairs([(cur[s], cur[s + 1]) for s in range(0, nseq, 2)], kf[o], consts, nr)
            conv = [t for pair in conv for t in pair]
            gate = x1 if o == 0 else x2
            cur = [gate[s] * (conv[s] + cur[s] * d) for s in range(nseq)]
        for s in range(nseq):
            o_ref[s, i] = cur[s]


def hyena_conv(seqs, kern, conv_w, conv_b, d_skip, ngroups, h, ctx_rows):
    nseq, ch = seqs.shape[0], seqs.shape[1]
    ns = nseq * ngroups
    nr = 2 * h
    cst = _dft_constants(nr)
    p = jnp.zeros((8, ch), F32).at[0:3].set(conv_w).at[3].set(conv_b).at[4:6, 0:D_GROUP].set(d_skip)
    ncb = D_GROUP // HY_CB
    seq_spec = lambda part: pl.BlockSpec((nseq, HY_CB, h, LANES), lambda g, j: (0, part * ncb + j, g, 0))
    return pl.pallas_call(
        functools.partial(_hy_conv_kernel, nseq=nseq, nr=nr, ctx_rows=ctx_rows),
        grid=(ngroups, ncb),
        in_specs=[pl.BlockSpec(memory_space=pltpu.SMEM), seq_spec(0), seq_spec(1), seq_spec(2),
                  pl.BlockSpec((1, HY_ORDER, HY_CB, nr, LANES), lambda g, j: (g, 0, j, 0, 0)),
                  _full(2 * nr, h), _full(2 * nr, nr), _full(nr, nr), _full(LANES, 2 * LANES), _full(2, nr, LANES)],
        out_specs=pl.BlockSpec((nseq, HY_CB, h, LANES), lambda g, j: (g, j, 0, 0)),
        out_shape=jax.ShapeDtypeStruct((ns, D_GROUP, h, LANES), F32),
        compiler_params=_params("arbitrary", "arbitrary"),
        name="hyena_conv",
    )(p, seqs, seqs, seqs, kern, cst["ab_half"], cst["ab_full"], cst["atbt"], cst["cc"], cst["tw"])


def _hy_filter_kernel(len_ref, w1t_ref, b1_ref, w2t_ref, b2_ref, w3t_ref, b3_ref, w4t_ref, f_ref, bands_ref,
                      deltas_ref, o_ref, *, nr):
    g, jb = pl.program_id(0), pl.program_id(1)
    l_ = len_ref[g]
    lf = l_.astype(F32)
    n = nr * LANES
    w = HY_FROWS * LANES
    tau = (jb * w + _iota((1, w), 1))
    backward = tau > n // 2
    posi = jnp.where(backward, n - tau, tau)
    ok = jnp.where(backward, jnp.where(posi < l_, 1.0, 0.0), jnp.where(tau < l_, 1.0, 0.0))
    pos = posi.astype(F32)
    t = pos / jnp.maximum(lf - 1.0, 1.0)
    ang = (2.0 * math.pi / lf) * pos * bands_ref[...]
    w1t = w1t_ref[...]
    f = f_ref[...]
    hid = (w1t[:, 0:1] * t + _dot(w1t[:, 1:1 + HY_BANDS], jnp.cos(ang), prec=HIGHEST)
           + _dot(w1t[:, 1 + HY_BANDS:HY_EMB], -jnp.sin(ang), prec=HIGHEST))
    hid = jnp.sin(f[:, 0:1] * (hid + b1_ref[...]))
    hid = jnp.sin(f[:, 1:2] * (_dot(w2t_ref[...], hid, prec=HIGHEST) + b2_ref[...]))
    hid = jnp.sin(f[:, 2:3] * (_dot(w3t_ref[...], hid, prec=HIGHEST) + b3_ref[...]))
    out = _dot(w4t_ref[0], hid, prec=HIGHEST)
    window = jnp.exp(-t * deltas_ref[...]) * ok
    out = out * jnp.concatenate([window] * HY_ORDER, axis=0)
    for r in range(HY_FROWS):
        o_ref[0, r] = out[:, r * LANES:(r + 1) * LANES]


def hyena_filter_kernels(lens, w1, b1, w2, b2, w3, b3, w4, freq, nr):
    ng = lens.shape[0]
    hf = HY_FILTER_HIDDEN
    oc = HY_ORDER * D_GROUP
    w4t = w4.reshape(hf, HY_ORDER, N_DIR, D_GROUP).transpose(2, 1, 3, 0).reshape(N_DIR, oc, hf)
    bands = jnp.linspace(1e-4, HY_BANDS - 1, HY_BANDS, dtype=F32).reshape(HY_BANDS, 1)
    max_decay = math.log(HY_DECAY_TARGET) / HY_FAST_DECAY
    min_decay = math.log(HY_DECAY_TARGET) / HY_SLOW_DECAY
    deltas = jnp.abs(jnp.linspace(min_decay, max_decay, D_GROUP, dtype=F32)).reshape(D_GROUP, 1)
    nblk = nr // HY_FROWS
    col = lambda v: v.reshape(-1, 1)
    return pl.pallas_call(
        functools.partial(_hy_filter_kernel, nr=nr),
        grid=(ng, nblk),
        in_specs=[pl.BlockSpec(memory_space=pltpu.SMEM),
                  _full(hf, HY_EMB), _full(hf, 1), _full(hf, hf), _full(hf, 1), _full(hf, hf), _full(hf, 1),
                  pl.BlockSpec((1, oc, hf), lambda g, j: ((2 * j) // nblk, 0, 0)),
                  _full(hf, 3), _full(HY_BANDS, 1), _full(D_GROUP, 1)],
        out_specs=pl.BlockSpec((1, HY_FROWS, oc, LANES), lambda g, j: (g, j, 0, 0)),
        out_shape=jax.ShapeDtypeStruct((ng, nr, oc, LANES), F32),
        compiler_params=_params("arbitrary", "arbitrary"),
        name="hyena_filters",
    )(lens, w1.T, col(b1), w2.T, col(b2), w3.T, col(b3), w4t, freq.T, bands, deltas)


def hyena_mixer(u_t, seq, lc, conv_w, conv_b, w1, b1, w2, b2, w3, b3, w4, freq, d_skip, need_ctx):
    nb = u_t.shape[0]
    h = seq // LANES
    nr = 2 * h
    lens = [seq, lc] if need_ctx else [seq]
    ng = len(lens)
    kern = hyena_filter_kernels(jnp.asarray(lens, jnp.int32), w1, b1, w2, b2, w3, b3, w4, freq, nr)
    kern = kern.reshape(ng, nr, HY_ORDER, D_GROUP, LANES).transpose(0, 2, 3, 1, 4)
    y = hyena_conv(u_t.reshape(nb, HY_COLS, -1, LANES), kern, conv_w, conv_b, d_skip, ng, h, lc // LANES)
    from_tiles = lambda t: t.reshape(t.shape[0], D_GROUP, -1).transpose(0, 2, 1)
    y_lat = from_tiles(y[:nb])
    y_ctx = from_tiles(y[nb:])[:, :lc] if need_ctx else jnp.zeros((nb, lc, D_GROUP), F32)
    return y_lat, y_ctx


def to_col_major(u, rows):
    b_, l_, ch = u.shape
    return u.reshape(b_, rows, GRID_W, ch).transpose(0, 2, 1, 3).reshape(b_, l_, ch)


def to_row_major(u, rows):
    b_, l_, ch = u.shape
    return u.reshape(b_, GRID_W, rows, ch).transpose(0, 2, 1, 3).reshape(b_, l_, ch)


def kernel(x, c, ctx, c_ctx, w_mod, b_mod, norm_g, w_in, w_out, gla_w2, gla_b, gla_norm,
           hy_conv_w, hy_conv_b, hy_w1, hy_b1, hy_w2, hy_b2, hy_w3, hy_b3, hy_w4, hy_freq, hy_d,
           mb_conv_w, mb_conv_b, mb_a_log, mb_dt_bias, mb_d, mb_norm,
           dn_conv_w, dn_a_log, dn_dt_bias, dn_norm,
           moe_router, moe_w_gate, moe_w_up, moe_w_down):
    b_, seq, d_ = x.shape
    lc = ctx.shape[1]
    rows = seq // GRID_W
    xall = jnp.concatenate([x, ctx], axis=1)
    cond = jnp.concatenate([c, c_ctx[None, :], jnp.zeros((HALO - b_ - 1, d_), F32)], axis=0)
    for i in range(DEPTH):
        need_ctx = i < DEPTH - 1
        mod = modulation_all(cond, w_mod[i], b_mod[i])[:b_ + 1]
        sh1, sc1, g1, sh2, sc2, g2 = jnp.split(mod, N_MOD, axis=-1)
        w_out_b = w_out[i].astype(BF16)

        ua, ub_t, uc, ud = norm_mod_proj(xall, norm_g[i, 0], sh1, sc1, *split_in_proj(w_in[i]), lc)
        ya = gla_mixer(ua, gla_w2[i], gla_b[i], gla_norm[i], lc)
        yb_l, yb_c = hyena_mixer(ub_t, seq, lc, hy_conv_w[i], hy_conv_b[i], hy_w1[i], hy_b1[i], hy_w2[i],
                                 hy_b2[i], hy_w3[i], hy_b3[i], hy_w4[i], hy_freq[i], hy_d[i], need_ctx)
        yb = jnp.concatenate([yb_l, yb_c], axis=1)
        uc = jnp.concatenate([to_col_major(uc[:, :seq], rows), uc[:, seq:]], axis=1)
        yc = mamba_mixer(uc, mb_conv_w[i], mb_conv_b[i], mb_a_log[i], mb_dt_bias[i], mb_d[i], mb_norm[i], lc)
        yc = jnp.concatenate([to_row_major(yc[:, :seq], rows), yc[:, seq:]], axis=1)
        yd = deltanet_mixer(ud, dn_conv_w[i], dn_a_log[i], dn_dt_bias[i], dn_norm[i], lc)
        xall = proj_norm_residual((ya, yb, yc, yd), xall, g1, norm_g[i, 1], w_out_b, lc)

        moe_w = (moe_w_gate, moe_w_up, moe_w_down, i)
        hx, afft = norm_mod_router(xall, norm_g[i, 2], sh2, sc2, moe_router[i], lc)
        x_l = moe_residual(hx, afft, xall, 0, seq, g2[:b_], norm_g[i, 3], *moe_w)
        if not need_ctx:
            return x_l
        x_c = moe_residual(hx, afft, xall, seq, lc, g2[b_:], norm_g[i, 3], *moe_w)
        xall = jnp.concatenate([x_l, x_c], axis=1)
    return xall[:, :seq]
```
